```python
import math
import jax
import jax.numpy as jnp
from jax import lax
import numpy as np

D_MODEL = 1024
BATCH = 4
SEQ = 4096
DEPTH = 4

GRID_W = 64
CTX_LEN = 256

SSD_HEADS = 16
SSD_HEAD_DIM = 64
SSD_INNER = SSD_HEADS * SSD_HEAD_DIM
SSD_GROUPS = 2
SSD_STATE = 128
SSD_CONV = 5
SSD_CHUNK = 128
SSD_XBC = SSD_INNER + 2 * SSD_GROUPS * SSD_STATE

HG_HEADS = 8
HG_DK = 128
HG_DV = D_MODEL // HG_HEADS
HG_INNER_K = HG_HEADS * HG_DK
HG_INNER_V = HG_HEADS * HG_DV
HG_CHUNK = 64

IN_SIZES = (SSD_INNER, SSD_XBC, 2 * SSD_HEADS, HG_INNER_K, 2 * HG_INNER_K, HG_INNER_V, HG_INNER_V)
IN_TOTAL = sum(IN_SIZES)
MIX_OUT = SSD_INNER + HG_INNER_V

CONF_KERNEL = 31
D_FF = 2816
FFN_CONV = 3

N_EVEN = (DEPTH + 1) // 2
N_ODD = DEPTH // 2
ALPHA = (2 * DEPTH) ** 0.25
BETA = (8 * DEPTH) ** -0.25
EPS = 1e-5
F32 = jnp.float32

kernel_name = 'hybrid_ssd_hgrn2_conformer_dit_trunk'


def _split(t, sizes):
    return jnp.split(t, np.cumsum(sizes)[:-1].tolist(), axis=-1)


def _flip(t, rev):
    return t[:, ::-1] if rev else t


def layer_norm(t, g, b):
    tf = t.astype(F32)
    mu = tf.mean(-1, keepdims=True)
    var = jnp.square(tf - mu).mean(-1, keepdims=True)
    return ((tf - mu) * lax.rsqrt(var + EPS) * g.astype(F32) + b.astype(F32)).astype(t.dtype)


def rms_norm(t, w, groups):
    tf = t.astype(F32)
    tg = tf.reshape(tf.shape[:-1] + (groups, tf.shape[-1] // groups))
    tg = tg * lax.rsqrt(jnp.square(tg).mean(-1, keepdims=True) + EPS)
    return tg.reshape(tf.shape) * w.astype(F32)


def modulate(t, shift, scale):
    return t * (1 + scale) + shift


def dwconv1d(t, w, b):
    k = w.shape[0]
    y = lax.conv_general_dilated(t, w[:, None, :].astype(t.dtype), (1,), [(k // 2, k // 2)],
                                 dimension_numbers=('NWC', 'WIO', 'NWC'), feature_group_count=t.shape[-1])
    return y + b.astype(t.dtype)


def dwconv2d(t, w, b, rows, cols):
    bsz, n, ch = t.shape
    k = w.shape[0]
    y = lax.conv_general_dilated(t.reshape(bsz, rows, cols, ch), w[:, :, None, :].astype(t.dtype), (1, 1),
                                 [(k // 2, k // 2), (k // 2, k // 2)],
                                 dimension_numbers=('NHWC', 'HWIO', 'NHWC'), feature_group_count=ch)
    return y.reshape(bsz, n, ch) + b.astype(t.dtype)


def segsum(a):
    t = a.shape[-1]
    cs = jnp.cumsum(a, -1)
    mask = jnp.tril(jnp.ones((t, t), dtype=bool))
    return jnp.where(mask, cs[..., :, None] - cs[..., None, :], -jnp.inf)


def ssd_scan(xs, dt, a_coef, bm, cm, h0):
    b, l, h, p = xs.shape
    g, n = bm.shape[2:]
    j = h // g
    c = SSD_CHUNK
    nc = l // c
    xdt = (xs.astype(F32) * dt[..., None]).reshape(b, nc, c, g, j, p)
    a = (dt * a_coef).reshape(b, nc, c, h).transpose(0, 3, 1, 2)
    bc = bm.astype(F32).reshape(b, nc, c, g, n)
    cc = cm.astype(F32).reshape(b, nc, c, g, n)
    a_cum = jnp.cumsum(a, -1)
    decay = jnp.exp(segsum(a)).reshape(b, g, j, nc, c, c)
    cb = jnp.einsum('bclgn,bcsgn->bgcls', cc, bc)
    y_diag = jnp.einsum('bgcls,bgjcls,bcsgjp->bclgjp', cb, decay, xdt)
    decay_states = jnp.exp(a_cum[..., -1:] - a_cum).reshape(b, g, j, nc, c)
    states = jnp.einsum('bcsgn,bgjcs,bcsgjp->bcgjpn', bc, decay_states, xdt).reshape(b, nc, h, p, n)
    states = jnp.concatenate([h0[:, None].astype(F32), states], axis=1)
    decay_chunk = jnp.exp(segsum(jnp.pad(a_cum[..., -1], ((0, 0), (0, 0), (1, 0)))))
    states = jnp.einsum('bhzc,bchpn->bzhpn', decay_chunk, states)
    start, final = states[:, :-1], states[:, -1]
    y_off = jnp.einsum('bclgn,bcgjpn,bgjcl->bclgjp', cc, start.reshape(b, nc, g, j, p, n),
                       jnp.exp(a_cum).reshape(b, g, j, nc, c))
    return (y_diag + y_off).reshape(b, l, h, p), final


def gla_scan(q, k, v, log_f, s0):
    b, l, h, dk = q.shape
    dv = v.shape[-1]
    c = HG_CHUNK
    nc = l // c
    q, k, v, log_f = [t.astype(F32).reshape(b, nc, c, h, t.shape[-1]) for t in (q, k, v, log_f)]

    def step(s, inp):
        q_t, k_t, v_t, lf_t = inp
        s = jnp.exp(lf_t)[..., None] * s + k_t[..., None] * v_t[..., None, :]
        return s, jnp.einsum('bchk,bchkv->bchv', q_t, s)

    s_loc, o_loc = lax.scan(step, jnp.zeros((b, nc, h, dk, dv), F32),
                            tuple(jnp.moveaxis(t, 2, 0) for t in (q, k, v, log_f)))
    o_loc = jnp.moveaxis(o_loc, 0, 2)
    a_cum = jnp.cumsum(log_f, axis=2)

    def chunk_step(s, inp):
        s_end_local, a_tot = inp
        return jnp.exp(a_tot)[..., None] * s + s_end_local, s

    s_final, s_start = lax.scan(chunk_step, s0.astype(F32),
                                (jnp.moveaxis(s_loc, 1, 0), jnp.moveaxis(a_cum[:, :, -1], 1, 0)))
    s_start = jnp.moveaxis(s_start, 0, 1)
    o = o_loc + jnp.einsum('bcthk,bchkv->bcthv', q * jnp.exp(a_cum), s_start)
    return o.reshape(b, l, h, dv), s_final


def hgrn_gates(f_logit, lb, first):
    a = f_logit.astype(F32)
    if first:
        return jax.nn.log_sigmoid(a), jax.nn.sigmoid(-a)
    lb = lb.reshape(HG_HEADS, HG_DK)
    return jnp.log(lb + (1 - lb) * jax.nn.sigmoid(a)), (1 - lb) * jax.nn.sigmoid(-a)


def even_mixer(h, hc, w_in, conv_w, conv_b, dt_bias, a_log, d_skip, ssd_nw, lb, first, hg_nw, w_out, ctx_out):
    def branch_inputs(t):
        bsz, L, _ = t.shape
        z, xbc, dt_raw, q, f, v, g = _split(t @ w_in, IN_SIZES)
        xbc = jax.nn.silu(dwconv1d(xbc, conv_w, conv_b))
        xs, bm, cm = _split(xbc, (SSD_INNER, SSD_GROUPS * SSD_STATE, SSD_GROUPS * SSD_STATE))
        return {'z': z, 'g': g,
                'dt': dt_raw.astype(F32).reshape(bsz, L, 2, SSD_HEADS),
                'xs': xs.reshape(bsz, L, SSD_HEADS, SSD_HEAD_DIM),
                'bm': bm.reshape(bsz, L, SSD_GROUPS, SSD_STATE),
                'cm': cm.reshape(bsz, L, SSD_GROUPS, SSD_STATE),
                'q': jax.nn.silu(q).reshape(bsz, L, HG_HEADS, HG_DK),
                'f': f.reshape(bsz, L, 2, HG_HEADS, HG_DK),
                'v': v.reshape(bsz, L, HG_HEADS, HG_DV)}

    lat, cx = branch_inputs(h), branch_inputs(hc)
    bsz = h.shape[0]
    dsk = d_skip.astype(F32)[:, None]
    y_lat, y_ctx = dsk * lat['xs'].astype(F32), dsk * cx['xs'].astype(F32)
    o_lat_parts, o_ctx_parts = [], []
    for d in range(2):
        rev = d == 1
        a_coef = -jnp.exp(a_log[d].astype(F32))
        dt_l = jax.nn.softplus(lat['dt'][:, :, d] + dt_bias[d].astype(F32))
        dt_c = jax.nn.softplus(cx['dt'][:, :, d] + dt_bias[d].astype(F32))
        zeros_ssd = jnp.zeros((bsz, SSD_HEADS, SSD_HEAD_DIM, SSD_STATE), F32)
        yc_d, s_ctx = ssd_scan(_flip(cx['xs'], rev), _flip(dt_c, rev), a_coef,
                               _flip(cx['bm'], rev), _flip(cx['cm'], rev), zeros_ssd)
        yl_d, _ = ssd_scan(_flip(lat['xs'], rev), _flip(dt_l, rev), a_coef,
                           _flip(lat['bm'], rev), _flip(lat['cm'], rev), s_ctx)
        y_lat = y_lat + _flip(yl_d, rev)
        y_ctx = y_ctx + _flip(yc_d, rev)
        lf_l, k_l = hgrn_gates(lat['f'][:, :, d], lb, first)
        lf_c, k_c = hgrn_gates(cx['f'][:, :, d], lb, first)
        zeros_hg = jnp.zeros((bsz, HG_HEADS, HG_DK, HG_DV), F32)
        oc_d, st_ctx = gla_scan(_flip(cx['q'], rev), _flip(k_c, rev), _flip(cx['v'], rev),
                                _flip(lf_c, rev), zeros_hg)
        ol_d, _ = gla_scan(_flip(lat['q'], rev), _flip(k_l, rev), _flip(lat['v'], rev),
                           _flip(lf_l, rev), st_ctx)
        o_lat_parts.append(_flip(ol_d, rev))
        o_ctx_parts.append(_flip(oc_d, rev))

    def merge(s, y, o):
        bsz_, L = s['z'].shape[:2]
        ys = rms_norm(y.reshape(bsz_, L, SSD_INNER) * jax.nn.silu(s['z'].astype(F32)), ssd_nw, SSD_GROUPS)
        os_ = rms_norm(o.reshape(bsz_, L, HG_INNER_V), hg_nw, HG_HEADS) * jax.nn.silu(s['g'].astype(F32))
        return jnp.concatenate([ys, os_], axis=-1).astype(h.dtype) @ w_out

    y = merge(lat, y_lat, o_lat_parts[0] + o_lat_parts[1])
    yc = merge(cx, y_ctx, o_ctx_parts[0] + o_ctx_parts[1]) if ctx_out else None
    return y, yc


def conformer_conv(t, w1, b1, dw_w, dw_b, ln_g, ln_b, w2, b2):
    a, gate = jnp.split(t @ w1 + b1, 2, axis=-1)
    u = dwconv1d(a * jax.nn.sigmoid(gate), dw_w, dw_b)
    u = jax.nn.silu(layer_norm(u, ln_g, ln_b))
    return u @ w2 + b2


def conv_glu_ffn(t, w_up, conv_w, conv_b, w_down, rows, cols):
    v, gate = jnp.split(t @ w_up, 2, axis=-1)
    gate = dwconv2d(gate, conv_w, conv_b, rows, cols)
    return (jax.nn.gelu(gate, approximate=False) * v) @ w_down


def setup_inputs(seed: int = 0) -> dict:
    key = jax.random.key(seed)
    keys = iter(jax.random.split(key, 32))

    def nrm(shape, scale):
        return jax.random.normal(next(keys), shape, F32) * scale

    def unif(shape, lo, hi):
        return jax.random.uniform(next(keys), shape, F32, lo, hi)

    d = D_MODEL
    inputs = {}
    inputs['x'] = nrm((BATCH, SEQ, d), 1.0)
    inputs['c'] = nrm((BATCH, d), 1.0)
    inputs['ctx'] = nrm((BATCH, CTX_LEN, d), 1.0)
    inputs['c_ctx'] = nrm((d,), 1.0)
    inputs['ada_w'] = nrm((DEPTH, d, 6 * d), d ** -0.5)
    inputs['ada_b'] = nrm((DEPTH, 6 * d), 0.02)
    inputs['post_ln_g'] = 1.0 + nrm((DEPTH, 2, d), 0.02)
    inputs['post_ln_b'] = nrm((DEPTH, 2, d), 0.02)
    inputs['mix_w_in'] = nrm((N_EVEN, d, IN_TOTAL), d ** -0.5)
    inputs['ssd_conv_w'] = nrm((N_EVEN, SSD_CONV, SSD_XBC), SSD_CONV ** -0.5)
    inputs['ssd_conv_b'] = nrm((N_EVEN, SSD_XBC), 0.02)
    dt0 = jnp.exp(unif((N_EVEN, 2, SSD_HEADS), math.log(1e-3), math.log(1e-1)))
    inputs['ssd_dt_bias'] = dt0 + jnp.log(-jnp.expm1(-dt0))
    inputs['ssd_a_log'] = jnp.log(unif((N_EVEN, 2, SSD_HEADS), 1.0, 16.0))
    inputs['ssd_d'] = 1.0 + nrm((N_EVEN, SSD_HEADS), 0.02)
    inputs['ssd_norm_w'] = 1.0 + nrm((N_EVEN, SSD_INNER), 0.02)
    inputs['hg_lb_raw'] = nrm((N_EVEN, HG_INNER_K), 0.5)
    inputs['hg_norm_w'] = 1.0 + nrm((N_EVEN, HG_INNER_V), 0.02)
    inputs['mix_w_out'] = nrm((N_EVEN, MIX_OUT, d), MIX_OUT ** -0.5 * BETA)
    inputs['conf_w1'] = nrm((N_ODD, d, 2 * d), d ** -0.5)
    inputs['conf_b1'] = nrm((N_ODD, 2 * d), 0.02)
    inputs['conf_dw_w'] = nrm((N_ODD, CONF_KERNEL, d), CONF_KERNEL ** -0.5)
    inputs['conf_dw_b'] = nrm((N_ODD, d), 0.02)
    inputs['conf_ln_g'] = 1.0 + nrm((N_ODD, d), 0.02)
    inputs['conf_ln_b'] = nrm((N_ODD, d), 0.02)
    inputs['conf_w2'] = nrm((N_ODD, d, d), d ** -0.5 * BETA)
    inputs['conf_b2'] = nrm((N_ODD, d), 0.02)
    inputs['ffn_w_up'] = nrm((DEPTH, d, 2 * D_FF), d ** -0.5)
    inputs['ffn_conv_w'] = nrm((DEPTH, FFN_CONV, FFN_CONV, D_FF), 1.0 / FFN_CONV)
    inputs['ffn_conv_b'] = nrm((DEPTH, D_FF), 0.02)
    inputs['ffn_w_down'] = nrm((DEPTH, D_FF, d), D_FF ** -0.5 * BETA)
    return inputs


def reference(x, c, ctx, c_ctx, ada_w, ada_b, post_ln_g, post_ln_b, mix_w_in, ssd_conv_w, ssd_conv_b,
              ssd_dt_bias, ssd_a_log, ssd_d, ssd_norm_w, hg_lb_raw, hg_norm_w, mix_w_out,
              conf_w1, conf_b1, conf_dw_w, conf_dw_b, conf_ln_g, conf_ln_b, conf_w2, conf_b2,
              ffn_w_up, ffn_conv_w, ffn_conv_b, ffn_w_down):
    rows = x.shape[1] // GRID_W
    lb_p = jax.nn.softmax(hg_lb_raw.astype(F32), axis=0)
    lb_all = jnp.cumsum(lb_p, axis=0) - lb_p[0]
    s_c = jax.nn.silu(c)
    s_ctx = jax.nn.silu(c_ctx)
    xc = ctx
    for l in range(DEPTH):
        m = (s_c @ ada_w[l] + ada_b[l]).reshape(-1, 6, 1, D_MODEL)
        mc = (s_ctx @ ada_w[l] + ada_b[l]).reshape(6, D_MODEL)
        ctx_next = any(j % 2 == 0 for j in range(l + 1, DEPTH))
        h = modulate(x, m[:, 0], m[:, 1])
        if l % 2 == 0:
            e = l // 2
            hc = modulate(xc, mc[0], mc[1])
            y, yc = even_mixer(h, hc, mix_w_in[e], ssd_conv_w[e], ssd_conv_b[e], ssd_dt_bias[e], ssd_a_log[e],
                               ssd_d[e], ssd_norm_w[e], lb_all[e], e == 0, hg_norm_w[e], mix_w_out[e], ctx_next)
        else:
            o = l // 2
            conf = (conf_w1[o], conf_b1[o], conf_dw_w[o], conf_dw_b[o], conf_ln_g[o], conf_ln_b[o],
                    conf_w2[o], conf_b2[o])
            y = conformer_conv(h, *conf)
            yc = conformer_conv(modulate(xc, mc[0], mc[1]), *conf) if ctx_next else None
        ffn = (ffn_w_up[l], ffn_conv_w[l], ffn_conv_b[l], ffn_w_down[l])
        x = layer_norm(ALPHA * x + m[:, 2] * y, post_ln_g[l, 0], post_ln_b[l, 0])
        x = layer_norm(ALPHA * x + m[:, 5] * conv_glu_ffn(modulate(x, m[:, 3], m[:, 4]), *ffn, rows, GRID_W),
                       post_ln_g[l, 1], post_ln_b[l, 1])
        if ctx_next:
            xc = layer_norm(ALPHA * xc + mc[2] * yc, post_ln_g[l, 0], post_ln_b[l, 0])
            xc = layer_norm(ALPHA * xc + mc[5] * conv_glu_ffn(modulate(xc, mc[3], mc[4]), *ffn, 1, xc.shape[1]),
                            post_ln_g[l, 1], post_ln_b[l, 1])
    return x
```

```python
import functools

import jax
import jax.numpy as jnp
from jax import lax
from jax.experimental import pallas as pl
from jax.experimental.pallas import tpu as pltpu

F32 = jnp.float32
BF16 = jnp.bfloat16
HIGHEST = lax.Precision.HIGHEST

D_MODEL = 1024
DEPTH = 4
GRID_W = 64
SSD_HEADS = 16
SSD_HEAD_DIM = 64
SSD_INNER = SSD_HEADS * SSD_HEAD_DIM
SSD_GROUPS = 2
SSD_STATE = 128
SSD_BC = SSD_GROUPS * SSD_STATE
SSD_XBC = SSD_INNER + 2 * SSD_BC
SSD_CHUNK = 128
HG_HEADS = 8
HG_DK = 128
HG_DV = D_MODEL // HG_HEADS
HG_INNER = HG_HEADS * HG_DK
HG_CHUNK = 64
HG_SUB = 8
D_FF = 2816
CONF_KERNEL = 31
ALPHA = (2 * DEPTH) ** 0.25
EPS = 1e-5
N_MOD = 6
MOD_ROWS = 8
DT_PAD = 128
LANES = 128
SUBLANES = 8
VMEM_LIMIT = 48 * 1024 * 1024

COL_Z, COL_Q, COL_F, COL_V, COL_G = 0, 1, 2, 4, 5
COL_XBC = 6 * D_MODEL
IN_MAIN = COL_XBC + SSD_XBC


def _params(*sem):
    return pltpu.CompilerParams(dimension_semantics=sem, vmem_limit_bytes=VMEM_LIMIT)


def _silu(t):
    return t * jax.nn.sigmoid(t)


def _softplus(t):
    return jnp.maximum(t, 0.0) + jnp.log1p(jnp.exp(-jnp.abs(t)))


def _res_ln(x, gate, y, g, b):
    r = ALPHA * x + gate * y
    mu = jnp.mean(r, axis=-1, keepdims=True)
    rc = r - mu
    var = jnp.mean(rc * rc, axis=-1, keepdims=True)
    return rc * lax.rsqrt(var + EPS) * g + b


def _mod_spec(k, ctx):
    if ctx:
        return pl.BlockSpec((1, 1, D_MODEL), lambda b, *_: ((MOD_ROWS // 2) * N_MOD + k, 0, 0))
    return pl.BlockSpec((1, 1, D_MODEL), lambda b, *_: (b * N_MOD + k, 0, 0))


def _ada_kernel(s_ref, w_ref, b_ref, o_ref):
    s = _silu(s_ref[...])
    o_ref[0] = jnp.dot(s, w_ref[0], precision=HIGHEST, preferred_element_type=F32) + b_ref[0]


def ada_table(s, ada_w, ada_b):
    depth, d, n = ada_w.shape
    tn = n // 4
    return pl.pallas_call(
        _ada_kernel,
        grid=(depth, n // tn),
        in_specs=[pl.BlockSpec((MOD_ROWS, d), lambda l, j: (0, 0)),
                  pl.BlockSpec((1, d, tn), lambda l, j: (l, 0, j)),
                  pl.BlockSpec((1, 1, tn), lambda l, j: (l, 0, j))],
        out_specs=pl.BlockSpec((1, MOD_ROWS, tn), lambda l, j: (l, 0, j)),
        out_shape=jax.ShapeDtypeStruct((depth, MOD_ROWS, n), F32),
        compiler_params=_params("parallel", "parallel"),
        name="ada_table",
    )(s, ada_w, ada_b.reshape(depth, 1, n))


def _mm_kernel(*refs, glu, has_bias, has_extra):
    it = iter(refs)
    x_ref, sh_ref, sc_ref, w_ref = next(it), next(it), next(it), next(it)
    wg_ref = next(it) if glu else None
    b_ref = next(it) if has_bias else None
    bg_ref = next(it) if (has_bias and glu) else None
    wx_ref = next(it) if has_extra else None
    o_ref = next(it)
    ox_ref = next(it) if has_extra else None
    xm_ref = next(it)

    @pl.when(pl.program_id(2) == 0)
    def _():
        xm = x_ref[0] * (1.0 + sc_ref[0]) + sh_ref[0]
        xm_ref[...] = xm.astype(BF16)
        if has_extra:
            ox_ref[0] = jnp.dot(xm_ref[...], wx_ref[...], preferred_element_type=F32)

    xm = xm_ref[...]
    acc = jnp.dot(xm, w_ref[...], preferred_element_type=F32)
    if has_bias:
        acc = acc + b_ref[...]
    if glu:
        gate = jnp.dot(xm, wg_ref[...], preferred_element_type=F32)
        if has_bias:
            gate = gate + bg_ref[...]
        acc = acc * jax.nn.sigmoid(gate)
    o_ref[0] = acc.astype(o_ref.dtype)


def mod_matmul(x, modl, ctx, k_shift, k_scale, w, *, n_out, glu=False, bias=None, w_extra=None, name):
    bsz, seq, d = x.shape
    tm = min(seq, 1024)
    tn = 512
    nj = n_out // tn
    in_specs = [pl.BlockSpec((1, tm, d), lambda b, i, j: (b, i, 0)),
                _mod_spec(k_shift, ctx), _mod_spec(k_scale, ctx),
                pl.BlockSpec((d, tn), lambda b, i, j: (0, j))]
    args = [x, modl, modl, w]
    if glu:
        in_specs.append(pl.BlockSpec((d, tn), lambda b, i, j: (0, j + nj)))
        args.append(w)
    if bias is not None:
        in_specs.append(pl.BlockSpec((1, tn), lambda b, i, j: (0, j)))
        args.append(bias)
        if glu:
            in_specs.append(pl.BlockSpec((1, tn), lambda b, i, j: (0, j + nj)))
            args.append(bias)
    out_shape = [jax.ShapeDtypeStruct((bsz, seq, n_out), F32)]
    out_specs = [pl.BlockSpec((1, tm, tn), lambda b, i, j: (b, i, j))]
    if w_extra is not None:
        in_specs.append(pl.BlockSpec((d, DT_PAD), lambda b, i, j: (0, 0)))
        args.append(w_extra)
        out_shape.append(jax.ShapeDtypeStruct((bsz, seq, DT_PAD), F32))
        out_specs.append(pl.BlockSpec((1, tm, DT_PAD), lambda b, i, j: (b, i, 0)))
    res = pl.pallas_call(
        functools.partial(_mm_kernel, glu=glu, has_bias=bias is not None, has_extra=w_extra is not None),
        grid=(bsz, seq // tm, nj),
        in_specs=in_specs, out_specs=out_specs, out_shape=out_shape,
        scratch_shapes=[pltpu.VMEM((tm, d), BF16)],
        compiler_params=_params("parallel", "parallel", "arbitrary"),
        name=name,
    )(*args)
    return res if w_extra is not None else res[0]


def _resln_kernel(*refs, has_bias):
    it = iter(refs)
    a_ref, w_ref = next(it), next(it)
    b_ref = next(it) if has_bias else None
    x_ref, gate_ref, g_ref, beta_ref, o_ref = next(it), next(it), next(it), next(it), next(it)
    y = jnp.dot(a_ref[0], w_ref[...], preferred_element_type=F32)
    if has_bias:
        y = y + b_ref[...]
    o_ref[0] = _res_ln(x_ref[0], gate_ref[0], y, g_ref[...], beta_ref[...])


def matmul_res_ln(a, w, bias, x, modl, ctx, k_gate, ln_g, ln_b, *, name):
    bsz, seq, k = a.shape
    d = x.shape[-1]
    tm = min(seq, 512)
    row = lambda b, i: (b, i, 0)
    const = lambda b, i: (0, 0)
    in_specs = [pl.BlockSpec((1, tm, k), row), pl.BlockSpec((k, d), const)]
    args = [a, w]
    if bias is not None:
        in_specs.append(pl.BlockSpec((1, d), const))
        args.append(bias)
    in_specs += [pl.BlockSpec((1, tm, d), row), _mod_spec(k_gate, ctx),
                 pl.BlockSpec((1, d), const), pl.BlockSpec((1, d), const)]
    args += [x, modl, ln_g, ln_b]
    return pl.pallas_call(
        functools.partial(_resln_kernel, has_bias=bias is not None),
        grid=(bsz, seq // tm),
        in_specs=in_specs,
        out_specs=pl.BlockSpec((1, tm, d), row),
        out_shape=jax.ShapeDtypeStruct((bsz, seq, d), F32),
        compiler_params=_params("parallel", "parallel"),
        name=name,
    )(*args)


def _fill_halo(buf_ref, cur_ref, prev_ref, next_ref, halo, tl):
    i = pl.program_id(1)
    last = pl.num_programs(1) - 1
    buf_ref[0:halo] = jnp.where(i > 0, prev_ref[0], 0.0)
    buf_ref[halo:halo + tl] = cur_ref[0]
    buf_ref[halo + tl:halo + tl + halo] = jnp.where(i < last, next_ref[0], 0.0)


def _halo_specs(seq, tl, halo, width, col_block):
    per = tl // halo
    last = seq // halo - 1
    def prev_map(b, i, *rest):
        return (b, jnp.maximum(i * per - 1, 0), col_block(*rest))
    def next_map(b, i, *rest):
        return (b, jnp.minimum((i + 1) * per, last), col_block(*rest))
    return pl.BlockSpec((1, halo, width), prev_map), pl.BlockSpec((1, halo, width), next_map)


def _xbc_conv_kernel(cur_ref, prev_ref, next_ref, w_ref, b_ref, o_ref, buf_ref, *, taps, halo, tl):
    _fill_halo(buf_ref, cur_ref, prev_ref, next_ref, halo, tl)
    acc = jnp.broadcast_to(b_ref[...], (tl, b_ref.shape[-1]))
    for k in range(taps):
        off = halo - taps // 2 + k
        acc = acc + w_ref[k:k + 1, :] * buf_ref[off:off + tl, :]
    o_ref[0] = _silu(acc)


def xbc_conv(proj, conv_w, conv_b):
    bsz, seq, _ = proj.shape
    taps, width = conv_w.shape
    tc = 512
    tl = min(seq, 512)
    halo = SUBLANES
    first = COL_XBC // tc
    prev_spec, next_spec = _halo_specs(seq, tl, halo, tc, lambda j: first + j)
    return pl.pallas_call(
        functools.partial(_xbc_conv_kernel, taps=taps, halo=halo, tl=tl),
        grid=(bsz, seq // tl, width // tc),
        in_specs=[pl.BlockSpec((1, tl, tc), lambda b, i, j: (b, i, first + j)), prev_spec, next_spec,
                  pl.BlockSpec((taps, tc), lambda b, i, j: (0, j)),
                  pl.BlockSpec((1, tc), lambda b, i, j: (0, j))],
        out_specs=pl.BlockSpec((1, tl, tc), lambda b, i, j: (b, i, j)),
        out_shape=jax.ShapeDtypeStruct((bsz, seq, width), F32),
        scratch_shapes=[pltpu.VMEM((tl + 2 * halo, tc), F32)],
        compiler_params=_params("parallel", "parallel", "parallel"),
        name="xbc_conv",
    )(proj, proj, proj, conv_w, conv_b)


def _conf_conv_kernel(cur_ref, prev_ref, next_ref, w_ref, b_ref, g_ref, beta_ref, o_ref, buf_ref, *, taps, halo, tl):
    _fill_halo(buf_ref, cur_ref, prev_ref, next_ref, halo, tl)
    acc = jnp.broadcast_to(b_ref[...], (tl, b_ref.shape[-1]))
    for k in range(taps):
        off = halo - taps // 2 + k
        acc = acc + w_ref[k:k + 1, :] * buf_ref[off:off + tl, :]
    mu = jnp.mean(acc, axis=-1, keepdims=True)
    ac = acc - mu
    var = jnp.mean(ac * ac, axis=-1, keepdims=True)
    u = ac * lax.rsqrt(var + EPS) * g_ref[...] + beta_ref[...]
    o_ref[0] = _silu(u).astype(o_ref.dtype)


def conf_conv(glu, dw_w, dw_b, ln_g, ln_b):
    bsz, seq, d = glu.shape
    taps = dw_w.shape[0]
    tl = min(seq, 128)
    halo = 2 * SUBLANES
    prev_spec, next_spec = _halo_specs(seq, tl, halo, d, lambda: 0)
    const = lambda b, i: (0, 0)
    return pl.pallas_call(
        functools.partial(_conf_conv_kernel, taps=taps, halo=halo, tl=tl),
        grid=(bsz, seq // tl),
        in_specs=[pl.BlockSpec((1, tl, d), lambda b, i: (b, i, 0)), prev_spec, next_spec,
                  pl.BlockSpec((taps, d), const), pl.BlockSpec((1, d), const),
                  pl.BlockSpec((1, d), const), pl.BlockSpec((1, d), const)],
        out_specs=pl.BlockSpec((1, tl, d), lambda b, i: (b, i, 0)),
        out_shape=jax.ShapeDtypeStruct((bsz, seq, d), BF16),
        scratch_shapes=[pltpu.VMEM((tl + 2 * halo, d), F32)],
        compiler_params=_params("parallel", "parallel"),
        name="conf_conv",
    )(glu, glu, glu, dw_w, dw_b, ln_g, ln_b)


def _ssd_kernel(xs_ref, b_ref, c_ref, dt_ref, dtt_ref, dtb_ref, dtbt_ref, alog_ref, alogt_ref, dsk_ref, s0_ref,
                y_ref, sf_ref, st_ref, *, rev, direction, add_skip):
    T = SSD_CHUNK
    H, P, N = SSD_HEADS, SSD_HEAD_DIM, SSD_STATE
    GW = SSD_INNER // SSD_GROUPS

    @pl.when(pl.program_id(1) == 0)
    def _():
        st_ref[...] = s0_ref[0]

    dt = _softplus(dt_ref[0][:, direction * H:(direction + 1) * H] + dtb_ref[...])
    a_col = dt * (-jnp.exp(alog_ref[...]))
    a_row = _softplus(dtt_ref[0] + dtbt_ref[...]) * (-jnp.exp(alogt_ref[...]))

    ti = lax.broadcasted_iota(jnp.int32, (T, T), 0)
    si = lax.broadcasted_iota(jnp.int32, (T, T), 1)
    valid = (si >= ti) if rev else (si <= ti)
    valid_f = valid.astype(F32)
    valid_t = ((ti >= si) if rev else (ti <= si)).astype(F32)
    acum = jnp.dot(valid_f, a_col, precision=HIGHEST, preferred_element_type=F32)
    acum_row = jnp.dot(a_row, valid_t, precision=HIGHEST, preferred_element_type=F32)
    atot = acum[0:1] if rev else acum[T - 1:T]

    e_p = (lax.broadcasted_iota(jnp.int32, (H, H * P), 1) // P
           == lax.broadcasted_iota(jnp.int32, (H, H * P), 0)).astype(F32)
    e_t = (lax.broadcasted_iota(jnp.int32, (H, H * T), 1) // T
           == lax.broadcasted_iota(jnp.int32, (H, H * T), 0)).astype(F32)
    expand = lambda t, e: jnp.dot(t, e, precision=HIGHEST, preferred_element_type=F32)
    dt_x = expand(dt, e_p)
    eac_x = expand(jnp.exp(acum), e_p)
    dec_x = expand(jnp.exp(atot - acum), e_p)
    eat_x = expand(jnp.broadcast_to(jnp.exp(atot), (SUBLANES, H)), e_p)[0:1]
    acum_x = expand(acum, e_t)

    xs = xs_ref[0]
    xdt = xs * dt_x
    xdec = (xdt * dec_x).astype(BF16)
    bm = b_ref[0].astype(BF16)
    cm = c_ref[0].astype(BF16)
    lane = lax.broadcasted_iota(jnp.int32, (T, 2 * P), 1)

    y_parts = []
    for g in range(SSD_GROUPS):
        bg = bm[:, g * N:(g + 1) * N]
        cg = cm[:, g * N:(g + 1) * N]
        cb = lax.dot_general(cg, bg, (((1,), (1,)), ((), ())), preferred_element_type=F32)
        s_g = st_ref[:, g * GW:(g + 1) * GW]
        y_off = jnp.dot(cg, s_g.astype(BF16), preferred_element_type=F32) * eac_x[:, g * GW:(g + 1) * GW]
        upd = lax.dot_general(bg, xdec[:, g * GW:(g + 1) * GW], (((0,), (0,)), ((), ())),
                              preferred_element_type=F32)
        st_ref[:, g * GW:(g + 1) * GW] = s_g * eat_x[:, g * GW:(g + 1) * GW] + upd
        heads_per_group = H // SSD_GROUPS
        pair_parts = []
        for pp in range(heads_per_group // 2):
            l_pair = []
            for h in (g * heads_per_group + 2 * pp, g * heads_per_group + 2 * pp + 1):
                diff = acum_x[:, h * T:(h + 1) * T] - acum_row[h:h + 1, :]
                l_pair.append(jnp.where(valid, jnp.exp(jnp.minimum(diff, 0.0)), 0.0) * cb)
            lhs = jnp.concatenate(l_pair, axis=1).astype(BF16)
            col0 = (g * heads_per_group + 2 * pp) * P
            xp = xdt[:, col0:col0 + 2 * P]
            rhs = jnp.concatenate([jnp.where(lane < P, xp, 0.0), jnp.where(lane >= P, xp, 0.0)],
                                  axis=0).astype(BF16)
            pair_parts.append(jnp.dot(lhs, rhs, preferred_element_type=F32))
        y_parts.append(jnp.concatenate(pair_parts, axis=1) + y_off)
    y = jnp.concatenate(y_parts, axis=1)
    if add_skip:
        y = y + dsk_ref[...] * xs
    y_ref[0] = y

    @pl.when(pl.program_id(1) == pl.num_programs(1) - 1)
    def _():
        sf_ref[0] = st_ref[...]


def ssd_scan(xbc, dt, dt_t, dt_bias, a_log, d_skip_x, s0, *, direction, add_skip):
    bsz, seq, _ = xbc.shape
    T, H = SSD_CHUNK, SSD_HEADS
    nc = seq // T
    rev = direction == 1
    chunk = (lambda c: nc - 1 - c) if rev else (lambda c: c)
    const = lambda b, c: (0, 0)
    y, sf = pl.pallas_call(
        functools.partial(_ssd_kernel, rev=rev, direction=direction, add_skip=add_skip),
        grid=(bsz, nc),
        in_specs=[pl.BlockSpec((1, T, SSD_INNER), lambda b, c: (b, chunk(c), 0)),
                  pl.BlockSpec((1, T, SSD_BC), lambda b, c: (b, chunk(c), SSD_INNER // SSD_BC)),
                  pl.BlockSpec((1, T, SSD_BC), lambda b, c: (b, chunk(c), SSD_INNER // SSD_BC + 1)),
                  pl.BlockSpec((1, T, DT_PAD), lambda b, c: (b, chunk(c), 0)),
                  pl.BlockSpec((1, H, T), lambda b, c: (b, direction, chunk(c))),
                  pl.BlockSpec((1, H), const), pl.BlockSpec((H, 1), const),
                  pl.BlockSpec((1, H), const), pl.BlockSpec((H, 1), const),
                  pl.BlockSpec((1, SSD_INNER), const),
                  pl.BlockSpec((1, SSD_STATE, SSD_INNER), lambda b, c: (b, 0, 0))],
        out_specs=[pl.BlockSpec((1, T, SSD_INNER), lambda b, c: (b, chunk(c), 0)),
                   pl.BlockSpec((1, SSD_STATE, SSD_INNER), lambda b, c: (b, 0, 0))],
        out_shape=[jax.ShapeDtypeStruct((bsz, seq, SSD_INNER), F32),
                   jax.ShapeDtypeStruct((bsz, SSD_STATE, SSD_INNER), F32)],
        scratch_shapes=[pltpu.VMEM((SSD_STATE, SSD_INNER), F32)],
        compiler_params=_params("parallel", "arbitrary"),
        name=f"ssd_scan_d{direction}",
    )(xbc, xbc, xbc, dt, dt_t, dt_bias[None, :], dt_bias[:, None], a_log[None, :], a_log[:, None], d_skip_x, s0)
    return y, sf


def _gla_kernel(q_ref, f_ref, v_ref, lbraw_ref, s0_ref, o_ref, sf_ref, st_ref, *, rev, first, layer):
    T, SB = HG_CHUNK, HG_SUB
    H, DK, DV = HG_HEADS, HG_DK, HG_DV
    nb = T // SB

    @pl.when(pl.program_id(1) == 0)
    def _():
        st_ref[...] = s0_ref[0]

    a = f_ref[0]
    if first:
        lf = jnp.minimum(a, 0.0) - jnp.log1p(jnp.exp(-jnp.abs(a)))
        kk = jax.nn.sigmoid(-a)
    else:
        raw = lbraw_ref[...]
        pr = jnp.exp(raw - jnp.max(raw, axis=0, keepdims=True))
        pr = pr / jnp.sum(pr, axis=0, keepdims=True)
        lb = jnp.sum(pr[0:layer + 1], axis=0, keepdims=True) - pr[0:1]
        lf = jnp.log(lb + (1.0 - lb) * jax.nn.sigmoid(a))
        kk = (1.0 - lb) * jax.nn.sigmoid(-a)
    q = _silu(q_ref[0])
    v = v_ref[0]
    v_bf = v.astype(BF16)

    ti = lax.broadcasted_iota(jnp.int32, (T, T), 0)
    si = lax.broadcasted_iota(jnp.int32, (T, T), 1)
    valid_f = ((si >= ti) if rev else (si <= ti)).astype(F32)
    acum = jnp.dot(valid_f, lf, precision=HIGHEST, preferred_element_type=F32)
    atot = acum[0:1] if rev else acum[T - 1:T]
    qe = (q * jnp.exp(acum)).astype(BF16)
    kd = (kk * jnp.exp(atot - acum)).astype(BF16)

    row = lax.broadcasted_iota(jnp.int32, (SB, 1), 0)
    head = lambda t, h: t[:, h * DK:(h + 1) * DK]
    o_blocks = []
    for i in range(nb):
        lo, hi = SB * i, SB * (i + 1)
        a_i, q_i, k_i, v_i = acum[lo:hi], q[lo:hi], kk[lo:hi], v[lo:hi]
        o_h = [jnp.zeros((SB, DV), F32) for _ in range(H)]
        for s in range(SB):
            dec = jnp.exp(jnp.minimum(a_i - a_i[s:s + 1], 0.0))
            w = q_i * (k_i[s:s + 1] * dec)
            ok = (row <= s) if rev else (row >= s)
            for h in range(H):
                p = jnp.where(ok, jnp.sum(head(w, h), axis=-1, keepdims=True), 0.0)
                o_h[h] = o_h[h] + p * head(v_i, h)[s:s + 1]
        if rev:
            e_lo, e_hi, ref_row = hi, T, hi
        else:
            e_lo, e_hi, ref_row = 0, lo, lo - 1
        if e_hi > e_lo:
            r = acum[ref_row:ref_row + 1]
            q_s = (q_i * jnp.exp(a_i - r)).astype(BF16)
            k_s = (kk[e_lo:e_hi] * jnp.exp(r - acum[e_lo:e_hi])).astype(BF16)
            for h in range(H):
                p = lax.dot_general(head(q_s, h), head(k_s, h), (((1,), (1,)), ((), ())),
                                    preferred_element_type=F32)
                o_h[h] = o_h[h] + jnp.dot(p.astype(BF16), head(v_bf, h)[e_lo:e_hi], preferred_element_type=F32)
        o_blocks.append(jnp.concatenate(o_h, axis=1))
    o = jnp.concatenate(o_blocks, axis=0)

    inter = []
    for h in range(H):
        s_t = st_ref[h]
        inter.append(lax.dot_general(head(qe, h), s_t.astype(BF16), (((1,), (1,)), ((), ())),
                                     preferred_element_type=F32))
        upd = lax.dot_general(head(v_bf, h), head(kd, h), (((0,), (0,)), ((), ())), preferred_element_type=F32)
        st_ref[h] = s_t * jnp.exp(head(atot, h)) + upd
    o_ref[0] = o + jnp.concatenate(inter, axis=1)

    @pl.when(pl.program_id(1) == pl.num_programs(1) - 1)
    def _():
        sf_ref[0] = st_ref[...]


def gla_scan(proj, lb_raw, s0, *, direction, first, layer):
    bsz, seq, _ = proj.shape
    T, H = HG_CHUNK, HG_HEADS
    nc = seq // T
    rev = direction == 1
    chunk = (lambda c: nc - 1 - c) if rev else (lambda c: c)
    o, sf = pl.pallas_call(
        functools.partial(_gla_kernel, rev=rev, first=first, layer=layer),
        grid=(bsz, nc),
        in_specs=[pl.BlockSpec((1, T, HG_INNER), lambda b, c: (b, chunk(c), COL_Q)),
                  pl.BlockSpec((1, T, HG_INNER), lambda b, c: (b, chunk(c), COL_F + direction)),
                  pl.BlockSpec((1, T, HG_INNER), lambda b, c: (b, chunk(c), COL_V)),
                  pl.BlockSpec(lb_raw.shape, lambda b, c: (0, 0)),
                  pl.BlockSpec((1, H, HG_DV, HG_DK), lambda b, c: (b, 0, 0, 0))],
        out_specs=[pl.BlockSpec((1, T, HG_INNER), lambda b, c: (b, chunk(c), 0)),
                   pl.BlockSpec((1, H, HG_DV, HG_DK), lambda b, c: (b, 0, 0, 0))],
        out_shape=[jax.ShapeDtypeStruct((bsz, seq, HG_INNER), F32),
                   jax.ShapeDtypeStruct((bsz, H, HG_DV, HG_DK), F32)],
        scratch_shapes=[pltpu.VMEM((H, HG_DV, HG_DK), F32)],
        compiler_params=_params("parallel", "arbitrary"),
        name=f"gla_scan_d{direction}",
    )(proj, proj, proj, lb_raw, s0)
    return o, sf


def _mixout_kernel(yf_ref, yb_ref, z_ref, of_ref, ob_ref, g_ref, nws_ref, nwh_ref, w_ref,
                   x_ref, gate_ref, lg_ref, lb_ref, o_ref):
    def group_rms(t, groups):
        width = t.shape[-1] // groups
        parts = []
        for gi in range(groups):
            tg = t[:, gi * width:(gi + 1) * width]
            ms = jnp.mean(tg * tg, axis=-1, keepdims=True)
            parts.append(tg * lax.rsqrt(ms + EPS))
        return jnp.concatenate(parts, axis=1)

    ys = group_rms((yf_ref[0] + yb_ref[0]) * _silu(z_ref[0]), SSD_GROUPS) * nws_ref[...]
    os_ = group_rms(of_ref[0] + ob_ref[0], HG_HEADS) * nwh_ref[...] * _silu(g_ref[0])
    a = jnp.concatenate([ys, os_], axis=1).astype(BF16)
    y = jnp.dot(a, w_ref[...], preferred_element_type=F32)
    o_ref[0] = _res_ln(x_ref[0], gate_ref[0], y, lg_ref[...], lb_ref[...])


def mix_out(y_f, y_b, o_f, o_b, proj, ssd_nw, hg_nw, w_out, x, modl, ctx, ln_g, ln_b):
    bsz, seq, d = x.shape
    tm = min(seq, 256)
    row = lambda b, i: (b, i, 0)
    const = lambda b, i: (0, 0)
    act = pl.BlockSpec((1, tm, d), row)
    return pl.pallas_call(
        _mixout_kernel,
        grid=(bsz, seq // tm),
        in_specs=[act, act, pl.BlockSpec((1, tm, d), lambda b, i: (b, i, COL_Z)), act, act,
                  pl.BlockSpec((1, tm, d), lambda b, i: (b, i, COL_G)),
                  pl.BlockSpec((1, d), const), pl.BlockSpec((1, d), const),
                  pl.BlockSpec(w_out.shape, const), act, _mod_spec(2, ctx),
                  pl.BlockSpec((1, d), const), pl.BlockSpec((1, d), const)],
        out_specs=act,
        out_shape=jax.ShapeDtypeStruct((bsz, seq, d), F32),
        compiler_params=_params("parallel", "parallel"),
        name="mix_out",
    )(y_f, y_b, proj, o_f, o_b, proj, ssd_nw, hg_nw, w_out, x, modl, ln_g, ln_b)


def _ffn_conv_kernel(v_ref, g_ref, w_ref, b_ref, o_ref, gp_ref, c0_ref, c2_ref, *, rows, cols, rb):
    seq = rows * cols
    pad = SUBLANES
    margin = pad + (cols if rows > 1 else 0)
    width = g_ref.shape[-1]
    gp_ref[0:margin] = jnp.zeros((margin, width), F32)
    gp_ref[margin + seq:margin + seq + margin] = jnp.zeros((margin, width), F32)
    gp_ref[margin:margin + seq] = g_ref[0]
    n = rb + 2 * pad
    for t0 in range(0, seq, rb):
        e0 = margin + t0 - pad
        mid = gp_ref[e0:e0 + n]
        col_sums = []
        for j in range(3):
            c = w_ref[3 + j:4 + j, :] * mid
            if rows > 1:
                c = c + w_ref[j:j + 1, :] * gp_ref[e0 - cols:e0 - cols + n]
                c = c + w_ref[6 + j:7 + j, :] * gp_ref[e0 + cols:e0 + cols + n]
            col_sums.append(c)
        c0_ref[...] = col_sums[0]
        c2_ref[...] = col_sums[2]
        col = (lax.broadcasted_iota(jnp.int32, (rb, 1), 0) + t0) & (cols - 1)
        u = (col_sums[1][pad:pad + rb] + b_ref[...]
             + jnp.where(col > 0, c0_ref[pad - 1:pad - 1 + rb], 0.0)
             + jnp.where(col < cols - 1, c2_ref[pad + 1:pad + 1 + rb], 0.0))
        gelu = 0.5 * u * (1.0 + lax.erf(u * (2.0 ** -0.5)))
        o_ref[0, t0:t0 + rb] = (gelu * v_ref[0, t0:t0 + rb]).astype(o_ref.dtype)


def ffn_conv(vg, conv_w, conv_b, rows, cols):
    bsz, seq, _ = vg.shape
    assert seq == rows * cols and cols & (cols - 1) == 0
    tc = 256
    nj = D_FF // tc
    rb = min(seq, 256)
    margin = SUBLANES + (cols if rows > 1 else 0)
    return pl.pallas_call(
        functools.partial(_ffn_conv_kernel, rows=rows, cols=cols, rb=rb),
        grid=(bsz, nj),
        in_specs=[pl.BlockSpec((1, seq, tc), lambda b, j: (b, 0, j)),
                  pl.BlockSpec((1, seq, tc), lambda b, j: (b, 0, j + nj)),
                  pl.BlockSpec((9, tc), lambda b, j: (0, j)),
                  pl.BlockSpec((1, tc), lambda b, j: (0, j))],
        out_specs=pl.BlockSpec((1, seq, tc), lambda b, j: (b, 0, j)),
        out_shape=jax.ShapeDtypeStruct((bsz, seq, D_FF), BF16),
        scratch_shapes=[pltpu.VMEM((seq + 2 * margin, tc), F32),
                        pltpu.VMEM((rb + 2 * SUBLANES, tc), F32),
                        pltpu.VMEM((rb + 2 * SUBLANES, tc), F32)],
        compiler_params=_params("parallel", "parallel"),
        name="ffn_conv",
    )(vg, vg, conv_w, conv_b)


def _ffn_block(x, modl, ctx, w_up, conv_w, conv_b, w_down, ln_g, ln_b, rows, cols):
    vg = mod_matmul(x, modl, ctx, 3, 4, w_up, n_out=2 * D_FF, name="ffn_up")
    act = ffn_conv(vg, conv_w, conv_b, rows, cols)
    return matmul_res_ln(act, w_down, None, x, modl, ctx, 5, ln_g, ln_b, name="ffn_down")


def _in_proj(x, modl, ctx, w_main, w_dt, conv_w, conv_b):
    proj, dt = mod_matmul(x, modl, ctx, 0, 1, w_main, n_out=IN_MAIN, w_extra=w_dt, name="in_proj")
    xbc = xbc_conv(proj, conv_w, conv_b)
    dt_t = jnp.swapaxes(dt[:, :, :2 * SSD_HEADS], 1, 2)
    return proj, xbc, dt, dt_t


def kernel(x, c, ctx, c_ctx, ada_w, ada_b, post_ln_g, post_ln_b, mix_w_in, ssd_conv_w, ssd_conv_b, ssd_dt_bias,
           ssd_a_log, ssd_d, ssd_norm_w, hg_lb_raw, hg_norm_w, mix_w_out, conf_w1, conf_b1, conf_dw_w, conf_dw_b,
           conf_ln_g, conf_ln_b, conf_w2, conf_b2, ffn_w_up, ffn_conv_w, ffn_conv_b, ffn_w_down):
    bsz, seq, d = x.shape
    rows = seq // GRID_W
    ctx_len = ctx.shape[1]
    assert d == D_MODEL and bsz <= MOD_ROWS // 2

    cond = jnp.zeros((MOD_ROWS, d), F32).at[:bsz].set(c).at[MOD_ROWS // 2].set(c_ctx)
    mod = ada_table(cond, ada_w, ada_b).reshape(DEPTH, MOD_ROWS * N_MOD, 1, d)

    xc = ctx
    for l in range(DEPTH):
        modl = mod[l]
        ctx_next = any(j % 2 == 0 for j in range(l + 1, DEPTH))
        ln_g0, ln_b0 = post_ln_g[l, 0][None], post_ln_b[l, 0][None]
        ln_g1, ln_b1 = post_ln_g[l, 1][None], post_ln_b[l, 1][None]
        if l % 2 == 0:
            e = l // 2
            w_in = mix_w_in[e]
            o_z, o_xbc, o_dt = 0, SSD_INNER, SSD_INNER + SSD_XBC
            o_q = o_dt + 2 * SSD_HEADS
            w_main = jnp.concatenate([w_in[:, o_z:o_xbc], w_in[:, o_q:], w_in[:, o_xbc:o_dt]], axis=1).astype(BF16)
            w_dt = jnp.pad(w_in[:, o_dt:o_q], ((0, 0), (0, DT_PAD - 2 * SSD_HEADS))).astype(BF16)
            conv_w, conv_b = ssd_conv_w[e], ssd_conv_b[e][None]
            d_skip_x = jnp.repeat(ssd_d[e], SSD_HEAD_DIM)[None]
            w_out = mix_w_out[e].astype(BF16)
            lat = _in_proj(x, modl, False, w_main, w_dt, conv_w, conv_b)
            cx = _in_proj(xc, modl, True, w_main, w_dt, conv_w, conv_b)
            ys, os_ = {}, {}
            for dr in range(2):
                scan_args = dict(direction=dr, add_skip=dr == 0)
                zeros_ssd = jnp.zeros((bsz, SSD_STATE, SSD_INNER), F32)
                ys["c", dr], s_ctx = ssd_scan(cx[1], cx[2], cx[3], ssd_dt_bias[e, dr], ssd_a_log[e, dr], d_skip_x,
                                              zeros_ssd, **scan_args)
                ys["l", dr], _ = ssd_scan(lat[1], lat[2], lat[3], ssd_dt_bias[e, dr], ssd_a_log[e, dr], d_skip_x,
                                          s_ctx, **scan_args)
                zeros_hg = jnp.zeros((bsz, HG_HEADS, HG_DV, HG_DK), F32)
                os_["c", dr], st_ctx = gla_scan(cx[0], hg_lb_raw, zeros_hg, direction=dr, first=e == 0, layer=e)
                os_["l", dr], _ = gla_scan(lat[0], hg_lb_raw, st_ctx, direction=dr, first=e == 0, layer=e)
            nws, nwh = ssd_norm_w[e][None], hg_norm_w[e][None]
            x1 = mix_out(ys["l", 0], ys["l", 1], os_["l", 0], os_["l", 1], lat[0], nws, nwh, w_out,
                         x, modl, False, ln_g0, ln_b0)
            if ctx_next:
                xc1 = mix_out(ys["c", 0], ys["c", 1], os_["c", 0], os_["c", 1], cx[0], nws, nwh, w_out,
                              xc, modl, True, ln_g0, ln_b0)
        else:
            o = l // 2
            w1 = conf_w1[o].astype(BF16)
            w2 = conf_w2[o].astype(BF16)
            conv_args = (conf_dw_w[o], conf_dw_b[o][None], conf_ln_g[o][None], conf_ln_b[o][None])

            def conformer(t, is_ctx):
                glu = mod_matmul(t, modl, is_ctx, 0, 1, w1, n_out=d, glu=True, bias=conf_b1[o][None], name="conf_glu")
                u = conf_conv(glu, *conv_args)
                return matmul_res_ln(u, w2, conf_b2[o][None], t, modl, is_ctx, 2, ln_g0, ln_b0, name="conf_out")

            x1 = conformer(x, False)
            if ctx_next:
                xc1 = conformer(xc, True)
        ffn_args = (ffn_w_up[l].astype(BF16), ffn_conv_w[l].reshape(9, D_FF), ffn_conv_b[l][None],
                    ffn_w_down[l].astype(BF16), ln_g1, ln_b1)
        x = _ffn_block(x1, modl, False, *ffn_args, rows, GRID_W)
        if ctx_next:
            xc = _ffn_block(xc1, modl, True, *ffn_args, 1, ctx_len)
    return x
```

```python
import functools

import jax
import jax.numpy as jnp
from jax import lax
from jax.experimental import pallas as pl
from jax.experimental.pallas import tpu as pltpu

F32 = jnp.float32
BF16 = jnp.bfloat16
HIGHEST = lax.Precision.HIGHEST

D_MODEL = 1024
DEPTH = 4
GRID_W = 64
SSD_HEADS = 16
SSD_HEAD_DIM = 64
SSD_INNER = SSD_HEADS * SSD_HEAD_DIM
SSD_GROUPS = 2
SSD_STATE = 128
SSD_BC = SSD_GROUPS * SSD_STATE
SSD_XBC = SSD_INNER + 2 * SSD_BC
SSD_CHUNK = 128
HG_HEADS = 8
HG_DK = 128
HG_DV = D_MODEL // HG_HEADS
HG_INNER = HG_HEADS * HG_DK
HG_CHUNK = 128
HG_SUB = 8
D_FF = 2816
CONF_KERNEL = 31
ALPHA = (2 * DEPTH) ** 0.25
EPS = 1e-5
LOG2E = 1.4426950408889634
N_MOD = 6
MOD_ROWS = 8
DT_PAD = 128
LANES = 128
SUBLANES = 8
VMEM_LIMIT = 48 * 1024 * 1024
MAX_TN = 1536

COL_Z, COL_Q, COL_F, COL_V, COL_G = 0, 1, 2, 4, 5
COL_XBC = 6 * D_MODEL
IN_MAIN = COL_XBC + SSD_XBC


def _params(*sem):
    return pltpu.CompilerParams(dimension_semantics=sem, vmem_limit_bytes=VMEM_LIMIT)


def _silu(t):
    return t * jax.nn.sigmoid(t)


def _softplus(t):
    return jnp.maximum(t, 0.0) + jnp.log1p(jnp.exp(-jnp.abs(t)))


def _split_dot(t, m, passes):
    acc, rem = None, t
    for i in range(passes):
        piece = rem.astype(BF16)
        part = jnp.dot(piece, m, preferred_element_type=F32)
        acc = part if acc is None else acc + part
        if i + 1 < passes:
            rem = rem - piece.astype(F32)
    return acc


def _res_ln(x, gate, y, g, b):
    r = ALPHA * x + gate * y
    mu = jnp.mean(r, axis=-1, keepdims=True)
    rc = r - mu
    var = jnp.mean(rc * rc, axis=-1, keepdims=True)
    return rc * lax.rsqrt(var + EPS) * g + b


def _mod_spec(k, ctx):
    if ctx:
        return pl.BlockSpec((1, 1, D_MODEL), lambda b, *_: ((MOD_ROWS // 2) * N_MOD + k, 0, 0))
    return pl.BlockSpec((1, 1, D_MODEL), lambda b, *_: (b * N_MOD + k, 0, 0))


def _ada_kernel(s_ref, w_ref, b_ref, o_ref):
    s = _silu(s_ref[...])
    o_ref[0] = jnp.dot(s, w_ref[0], precision=HIGHEST, preferred_element_type=F32) + b_ref[0]


def ada_table(s, ada_w, ada_b):
    depth, d, n = ada_w.shape
    tn = n // 4
    return pl.pallas_call(
        _ada_kernel,
        grid=(depth, n // tn),
        in_specs=[pl.BlockSpec((MOD_ROWS, d), lambda l, j: (0, 0)),
                  pl.BlockSpec((1, d, tn), lambda l, j: (l, 0, j)),
                  pl.BlockSpec((1, 1, tn), lambda l, j: (l, 0, j))],
        out_specs=pl.BlockSpec((1, MOD_ROWS, tn), lambda l, j: (l, 0, j)),
        out_shape=jax.ShapeDtypeStruct((depth, MOD_ROWS, n), F32),
        compiler_params=_params("parallel", "parallel"),
        name="ada_table",
    )(s, ada_w, ada_b.reshape(depth, 1, n))


def _mm_kernel(*refs, glu, has_bias, has_extra):
    it = iter(refs)
    x_ref, sh_ref, sc_ref, w_ref = next(it), next(it), next(it), next(it)
    wg_ref = next(it) if glu else None
    b_ref = next(it) if has_bias else None
    bg_ref = next(it) if (has_bias and glu) else None
    wx_ref = next(it) if has_extra else None
    o_ref = next(it)
    ox_ref = next(it) if has_extra else None
    xm_ref = next(it)

    @pl.when(pl.program_id(2) == 0)
    def _():
        xm = x_ref[0] * (1.0 + sc_ref[0]) + sh_ref[0]
        xm_ref[...] = xm.astype(BF16)
        if has_extra:
            ox_ref[0] = jnp.dot(xm_ref[...], wx_ref[...], preferred_element_type=F32)

    xm = xm_ref[...]
    acc = jnp.dot(xm, w_ref[...], preferred_element_type=F32)
    if has_bias:
        acc = acc + b_ref[...]
    if glu:
        gate = jnp.dot(xm, wg_ref[...], preferred_element_type=F32)
        if has_bias:
            gate = gate + bg_ref[...]
        acc = acc * jax.nn.sigmoid(gate)
    o_ref[0] = acc.astype(o_ref.dtype)


def mod_matmul(x, modl, ctx, k_shift, k_scale, w, *, n_out, glu=False, bias=None, w_extra=None, name):
    bsz, seq, d = x.shape
    tm = min(seq, 1024)
    tn = max(t for t in range(LANES, MAX_TN + 1, LANES) if n_out % t == 0)
    nj = n_out // tn
    in_specs = [pl.BlockSpec((1, tm, d), lambda b, i, j: (b, i, 0)),
                _mod_spec(k_shift, ctx), _mod_spec(k_scale, ctx),
                pl.BlockSpec((d, tn), lambda b, i, j: (0, j))]
    args = [x, modl, modl, w]
    if glu:
        in_specs.append(pl.BlockSpec((d, tn), lambda b, i, j: (0, j + nj)))
        args.append(w)
    if bias is not None:
        in_specs.append(pl.BlockSpec((1, tn), lambda b, i, j: (0, j)))
        args.append(bias)
        if glu:
            in_specs.append(pl.BlockSpec((1, tn), lambda b, i, j: (0, j + nj)))
            args.append(bias)
    out_shape = [jax.ShapeDtypeStruct((bsz, seq, n_out), F32)]
    out_specs = [pl.BlockSpec((1, tm, tn), lambda b, i, j: (b, i, j))]
    if w_extra is not None:
        in_specs.append(pl.BlockSpec((d, DT_PAD), lambda b, i, j: (0, 0)))
        args.append(w_extra)
        out_shape.append(jax.ShapeDtypeStruct((bsz, seq, DT_PAD), F32))
        out_specs.append(pl.BlockSpec((1, tm, DT_PAD), lambda b, i, j: (b, i, 0)))
    res = pl.pallas_call(
        functools.partial(_mm_kernel, glu=glu, has_bias=bias is not None, has_extra=w_extra is not None),
        grid=(bsz, seq // tm, nj),
        in_specs=in_specs, out_specs=out_specs, out_shape=out_shape,
        scratch_shapes=[pltpu.VMEM((tm, d), BF16)],
        compiler_params=_params("parallel", "parallel", "arbitrary"),
        name=name,
    )(*args)
    return res if w_extra is not None else res[0]


def _resln_kernel(*refs, has_bias):
    it = iter(refs)
    a_ref, w_ref = next(it), next(it)
    b_ref = next(it) if has_bias else None
    x_ref, gate_ref, g_ref, beta_ref, o_ref = next(it), next(it), next(it), next(it), next(it)
    y = jnp.dot(a_ref[0], w_ref[...], preferred_element_type=F32)
    if has_bias:
        y = y + b_ref[...]
    o_ref[0] = _res_ln(x_ref[0], gate_ref[0], y, g_ref[...], beta_ref[...])


def matmul_res_ln(a, w, bias, x, modl, ctx, k_gate, ln_g, ln_b, *, name):
    bsz, seq, k = a.shape
    d = x.shape[-1]
    tm = min(seq, 512)
    row = lambda b, i: (b, i, 0)
    const = lambda b, i: (0, 0)
    in_specs = [pl.BlockSpec((1, tm, k), row), pl.BlockSpec((k, d), const)]
    args = [a, w]
    if bias is not None:
        in_specs.append(pl.BlockSpec((1, d), const))
        args.append(bias)
    in_specs += [pl.BlockSpec((1, tm, d), row), _mod_spec(k_gate, ctx),
                 pl.BlockSpec((1, d), const), pl.BlockSpec((1, d), const)]
    args += [x, modl, ln_g, ln_b]
    return pl.pallas_call(
        functools.partial(_resln_kernel, has_bias=bias is not None),
        grid=(bsz, seq // tm),
        in_specs=in_specs,
        out_specs=pl.BlockSpec((1, tm, d), row),
        out_shape=jax.ShapeDtypeStruct((bsz, seq, d), F32),
        compiler_params=_params("parallel", "parallel"),
        name=name,
    )(*args)


def _fill_halo(buf_ref, cur_ref, prev_ref, next_ref, halo, tl):
    i = pl.program_id(1)
    last = pl.num_programs(1) - 1
    buf_ref[0:halo] = jnp.where(i > 0, prev_ref[0], 0.0)
    buf_ref[halo:halo + tl] = cur_ref[0]
    buf_ref[halo + tl:halo + tl + halo] = jnp.where(i < last, next_ref[0], 0.0)


def _halo_specs(seq, tl, halo, width, col_block):
    per = tl // halo
    last = seq // halo - 1
    def prev_map(b, i, *rest):
        return (b, jnp.maximum(i * per - 1, 0), col_block(*rest))
    def next_map(b, i, *rest):
        return (b, jnp.minimum((i + 1) * per, last), col_block(*rest))
    return pl.BlockSpec((1, halo, width), prev_map), pl.BlockSpec((1, halo, width), next_map)


def _xbc_conv_kernel(cur_ref, prev_ref, next_ref, w_ref, b_ref, o_ref, buf_ref, *, taps, halo, tl):
    _fill_halo(buf_ref, cur_ref, prev_ref, next_ref, halo, tl)
    acc = jnp.broadcast_to(b_ref[...], (tl, b_ref.shape[-1]))
    for k in range(taps):
        off = halo - taps // 2 + k
        acc = acc + w_ref[k:k + 1, :] * buf_ref[off:off + tl, :]
    o_ref[0] = _silu(acc)


def xbc_conv(proj, conv_w, conv_b):
    bsz, seq, _ = proj.shape
    taps, width = conv_w.shape
    tc = 512
    tl = min(seq, 512)
    halo = SUBLANES
    first = COL_XBC // tc
    prev_spec, next_spec = _halo_specs(seq, tl, halo, tc, lambda j: first + j)
    return pl.pallas_call(
        functools.partial(_xbc_conv_kernel, taps=taps, halo=halo, tl=tl),
        grid=(bsz, seq // tl, width // tc),
        in_specs=[pl.BlockSpec((1, tl, tc), lambda b, i, j: (b, i, first + j)), prev_spec, next_spec,
                  pl.BlockSpec((taps, tc), lambda b, i, j: (0, j)),
                  pl.BlockSpec((1, tc), lambda b, i, j: (0, j))],
        out_specs=pl.BlockSpec((1, tl, tc), lambda b, i, j: (b, i, j)),
        out_shape=jax.ShapeDtypeStruct((bsz, seq, width), F32),
        scratch_shapes=[pltpu.VMEM((tl + 2 * halo, tc), F32)],
        compiler_params=_params("parallel", "parallel", "parallel"),
        name="xbc_conv",
    )(proj, proj, proj, conv_w, conv_b)


def _conf_conv_kernel(cur_ref, prev_ref, next_ref, w_ref, b_ref, g_ref, beta_ref, o_ref, buf_ref, ph_ref,
                      *, taps, halo, tl):
    _fill_halo(buf_ref, cur_ref, prev_ref, next_ref, halo, tl)
    rows = tl + 2 * halo
    full = buf_ref[...]
    for p in range(1, SUBLANES):
        ph_ref[p - 1] = pltpu.roll(full, rows - p, axis=0)
    acc = jnp.broadcast_to(b_ref[...], (tl, b_ref.shape[-1]))
    for k in range(taps):
        off = halo - taps // 2 + k
        p = off % SUBLANES
        base = off - p
        win = buf_ref[base:base + tl, :] if p == 0 else ph_ref[p - 1, base:base + tl, :]
        acc = acc + w_ref[k:k + 1, :] * win
    mu = jnp.mean(acc, axis=-1, keepdims=True)
    ac = acc - mu
    var = jnp.mean(ac * ac, axis=-1, keepdims=True)
    u = ac * lax.rsqrt(var + EPS) * g_ref[...] + beta_ref[...]
    o_ref[0] = _silu(u).astype(o_ref.dtype)


def conf_conv(glu, dw_w, dw_b, ln_g, ln_b):
    bsz, seq, d = glu.shape
    taps = dw_w.shape[0]
    tl = min(seq, 128)
    halo = 2 * SUBLANES
    prev_spec, next_spec = _halo_specs(seq, tl, halo, d, lambda: 0)
    const = lambda b, i: (0, 0)
    return pl.pallas_call(
        functools.partial(_conf_conv_kernel, taps=taps, halo=halo, tl=tl),
        grid=(bsz, seq // tl),
        in_specs=[pl.BlockSpec((1, tl, d), lambda b, i: (b, i, 0)), prev_spec, next_spec,
                  pl.BlockSpec((taps, d), const), pl.BlockSpec((1, d), const),
                  pl.BlockSpec((1, d), const), pl.BlockSpec((1, d), const)],
        out_specs=pl.BlockSpec((1, tl, d), lambda b, i: (b, i, 0)),
        out_shape=jax.ShapeDtypeStruct((bsz, seq, d), BF16),
        scratch_shapes=[pltpu.VMEM((tl + 2 * halo, d), F32),
                        pltpu.VMEM((SUBLANES - 1, tl + 2 * halo, d), F32)],
        compiler_params=_params("parallel", "parallel"),
        name="conf_conv",
    )(glu, glu, glu, dw_w, dw_b, ln_g, ln_b)


def _ssd_kernel(xs_ref, b_ref, c_ref, dt_ref, dtt_ref, dtb_ref, dtbt_ref, alog_ref, alogt_ref, dsk_ref, s0_ref,
                y_ref, sf_ref, st_ref, *, rev, direction, add_skip):
    T = SSD_CHUNK
    H, P, N = SSD_HEADS, SSD_HEAD_DIM, SSD_STATE
    GW = SSD_INNER // SSD_GROUPS

    @pl.when(pl.program_id(1) == 0)
    def _():
        st_ref[...] = s0_ref[0]

    dt = _softplus(dt_ref[0][:, direction * H:(direction + 1) * H] + dtb_ref[...])
    a_col = dt * (-jnp.exp(alog_ref[...]))
    a_row = _softplus(dtt_ref[0] + dtbt_ref[...]) * (-jnp.exp(alogt_ref[...]))

    ti = lax.broadcasted_iota(jnp.int32, (T, T), 0)
    si = lax.broadcasted_iota(jnp.int32, (T, T), 1)
    valid = (si >= ti) if rev else (si <= ti)
    valid_f = valid.astype(F32)
    valid_t = ((ti >= si) if rev else (ti <= si)).astype(F32)
    acum = jnp.dot(valid_f, a_col, precision=HIGHEST, preferred_element_type=F32)
    acum_row = jnp.dot(a_row, valid_t, precision=HIGHEST, preferred_element_type=F32)
    atot = acum[0:1] if rev else acum[T - 1:T]

    e_p = (lax.broadcasted_iota(jnp.int32, (H, H * P), 1) // P
           == lax.broadcasted_iota(jnp.int32, (H, H * P), 0)).astype(BF16)
    expand = lambda t, passes: _split_dot(t, e_p, passes)
    dt_x = expand(dt, 2)
    eac_x = expand(jnp.exp(acum), 3)
    dec_x = expand(jnp.exp(atot - acum), 2)
    eat_x = expand(jnp.broadcast_to(jnp.exp(atot), (SUBLANES, H)), 3)[0:1]

    xs = xs_ref[0]
    xdt = xs * dt_x
    xdec = (xdt * dec_x).astype(BF16)
    bm = b_ref[0].astype(BF16)
    cm = c_ref[0].astype(BF16)
    lane = lax.broadcasted_iota(jnp.int32, (T, 2 * P), 1)

    y_parts = []
    for g in range(SSD_GROUPS):
        bg = bm[:, g * N:(g + 1) * N]
        cg = cm[:, g * N:(g + 1) * N]
        cb = lax.dot_general(cg, bg, (((1,), (1,)), ((), ())), preferred_element_type=F32)
        s_g = st_ref[:, g * GW:(g + 1) * GW]
        y_off = jnp.dot(cg, s_g.astype(BF16), preferred_element_type=F32) * eac_x[:, g * GW:(g + 1) * GW]
        upd = lax.dot_general(bg, xdec[:, g * GW:(g + 1) * GW], (((0,), (0,)), ((), ())),
                              preferred_element_type=F32)
        st_ref[:, g * GW:(g + 1) * GW] = s_g * eat_x[:, g * GW:(g + 1) * GW] + upd
        heads_per_group = H // SSD_GROUPS
        pair_parts = []
        for pp in range(heads_per_group // 2):
            l_pair = []
            for h in (g * heads_per_group + 2 * pp, g * heads_per_group + 2 * pp + 1):
                diff = acum[:, h:h + 1] - acum_row[h:h + 1, :]
                l_pair.append(jnp.where(valid, jnp.exp(jnp.minimum(diff, 0.0)), 0.0) * cb)
            lhs = jnp.concatenate(l_pair, axis=1).astype(BF16)
            col0 = (g * heads_per_group + 2 * pp) * P
            xp = xdt[:, col0:col0 + 2 * P]
            rhs = jnp.concatenate([jnp.where(lane < P, xp, 0.0), jnp.where(lane >= P, xp, 0.0)],
                                  axis=0).astype(BF16)
            pair_parts.append(jnp.dot(lhs, rhs, preferred_element_type=F32))
        y_parts.append(jnp.concatenate(pair_parts, axis=1) + y_off)
    y = jnp.concatenate(y_parts, axis=1)
    if add_skip:
        y = y + dsk_ref[...] * xs
    y_ref[0] = y

    @pl.when(pl.program_id(1) == pl.num_programs(1) - 1)
    def _():
        sf_ref[0] = st_ref[...]


def ssd_scan(xbc, dt, dt_t, dt_bias, a_log, d_skip_x, s0, *, direction, add_skip):
    bsz, seq, _ = xbc.shape
    T, H = SSD_CHUNK, SSD_HEADS
    nc = seq // T
    rev = direction == 1
    chunk = (lambda c: nc - 1 - c) if rev else (lambda c: c)
    const = lambda b, c: (0, 0)
    y, sf = pl.pallas_call(
        functools.partial(_ssd_kernel, rev=rev, direction=direction, add_skip=add_skip),
        grid=(bsz, nc),
        in_specs=[pl.BlockSpec((1, T, SSD_INNER), lambda b, c: (b, chunk(c), 0)),
                  pl.BlockSpec((1, T, SSD_BC), lambda b, c: (b, chunk(c), SSD_INNER // SSD_BC)),
                  pl.BlockSpec((1, T, SSD_BC), lambda b, c: (b, chunk(c), SSD_INNER // SSD_BC + 1)),
                  pl.BlockSpec((1, T, DT_PAD), lambda b, c: (b, chunk(c), 0)),
                  pl.BlockSpec((1, H, T), lambda b, c: (b, direction, chunk(c))),
                  pl.BlockSpec((1, H), const), pl.BlockSpec((H, 1), const),
                  pl.BlockSpec((1, H), const), pl.BlockSpec((H, 1), const),
                  pl.BlockSpec((1, SSD_INNER), const),
                  pl.BlockSpec((1, SSD_STATE, SSD_INNER), lambda b, c: (b, 0, 0))],
        out_specs=[pl.BlockSpec((1, T, SSD_INNER), lambda b, c: (b, chunk(c), 0)),
                   pl.BlockSpec((1, SSD_STATE, SSD_INNER), lambda b, c: (b, 0, 0))],
        out_shape=[jax.ShapeDtypeStruct((bsz, seq, SSD_INNER), F32),
                   jax.ShapeDtypeStruct((bsz, SSD_STATE, SSD_INNER), F32)],
        scratch_shapes=[pltpu.VMEM((SSD_STATE, SSD_INNER), F32)],
        compiler_params=_params("parallel", "arbitrary"),
        name=f"ssd_scan_d{direction}",
    )(xbc, xbc, xbc, dt, dt_t, dt_bias[None, :], dt_bias[:, None], a_log[None, :], a_log[:, None], d_skip_x, s0)
    return y, sf


def _gla_kernel(q_ref, f_ref, v_ref, lbraw_ref, s0_ref, o_ref, sf_ref, st_ref, c_scr, *, rev, first, layer):
    T, SB = HG_CHUNK, HG_SUB
    H, DK, DV = HG_HEADS, HG_DK, HG_DV
    W = H * DK
    NT = (((1,), (1,)), ((), ()))
    TN = (((0,), (0,)), ((), ()))

    @pl.when(pl.program_id(1) == 0)
    def _():
        st_ref[...] = s0_ref[0]

    a = f_ref[0]
    soft = jnp.log1p(jnp.exp(-jnp.abs(a)))
    log_sig_neg = jnp.minimum(-a, 0.0) - soft
    if first:
        lf = jnp.minimum(a, 0.0) - soft
        kk = jax.nn.sigmoid(-a)
        log_k = log_sig_neg
    else:
        raw = lbraw_ref[...]
        pr = jnp.exp(raw - jnp.max(raw, axis=0, keepdims=True))
        pr = pr / jnp.sum(pr, axis=0, keepdims=True)
        lb = jnp.sum(pr[0:layer + 1], axis=0, keepdims=True) - pr[0:1]
        lf = jnp.log(lb + (1.0 - lb) * jax.nn.sigmoid(a))
        kk = (1.0 - lb) * jax.nn.sigmoid(-a)
        log_k = jnp.log(1.0 - lb) + log_sig_neg
    q = _silu(q_ref[0])
    v_bf = v_ref[0].astype(BF16)

    ti = lax.broadcasted_iota(jnp.int32, (T, T), 0)
    si = lax.broadcasted_iota(jnp.int32, (T, T), 1)
    valid_f = ((si >= ti) if rev else (si <= ti)).astype(F32)
    acum = jnp.dot(valid_f, lf * LOG2E, precision=HIGHEST, preferred_element_type=F32)
    atot = acum[0:1] if rev else acum[T - 1:T]
    qe = (q * jnp.exp2(acum)).astype(BF16)
    kd = (kk * jnp.exp2(atot - acum)).astype(BF16)
    c_scr[...] = acum - log_k * LOG2E
    head = lambda t, h: t[:, h * DK:(h + 1) * DK]

    level_p = []
    b = T // 2
    while b >= SB:
        q_parts, k_parts = [], []
        zeros = jnp.zeros((b, W), F32)
        for c0 in range(0, T, 2 * b):
            lo, hi = slice(c0, c0 + b), slice(c0 + b, c0 + 2 * b)
            src, tgt = (hi, lo) if rev else (lo, hi)
            ref_row = c0 + b if rev else c0 + b - 1
            r = acum[ref_row:ref_row + 1]
            q_t = q[tgt] * jnp.exp2(acum[tgt] - r)
            k_s = kk[src] * jnp.exp2(r - acum[src])
            q_parts += [q_t, zeros] if rev else [zeros, q_t]
            k_parts += [zeros, k_s] if rev else [k_s, zeros]
        q_l = jnp.concatenate(q_parts, axis=0).astype(BF16)
        k_l = jnp.concatenate(k_parts, axis=0).astype(BF16)
        level_p.append([lax.dot_general(head(q_l, h), head(k_l, h), NT, preferred_element_type=F32)
                        for h in range(H)])
        b //= 2

    inter = [lax.dot_general(head(qe, h), st_ref[h].astype(BF16), NT, preferred_element_type=F32) for h in range(H)]
    upd = [lax.dot_general(head(v_bf, h), head(kd, h), TN, preferred_element_type=F32) for h in range(H)]
    for h in range(H):
        st_ref[h] = st_ref[h] * jnp.exp2(head(atot, h)) + upd[h]

    row = lax.broadcasted_iota(jnp.int32, (SB, 1), 0)
    lane = lax.broadcasted_iota(jnp.int32, (SB, T), 1)
    diag_p = [[] for _ in range(H)]
    for lo in range(0, T, SB):
        a_i, q_i = acum[lo:lo + SB], q[lo:lo + SB]
        p_h = [jnp.zeros((SB, T), F32) for _ in range(H)]
        for s in range(SB):
            c_s = jnp.broadcast_to(c_scr[lo + s:lo + s + 1, :], (SB, W))
            ok = (row <= s) if rev else (row >= s)
            w = q_i * jnp.exp2(jnp.where(ok, a_i - c_s, -1e30))
            here = lane == lo + s
            for h in range(H):
                p_h[h] = jnp.where(here, jnp.sum(head(w, h), axis=-1, keepdims=True), p_h[h])
        for h in range(H):
            diag_p[h].append(p_h[h])

    same = []
    b = T // 4
    while b >= SB:
        shift = (2 * b).bit_length() - 1
        same.append((ti >> shift) == (si >> shift))
        b //= 2
    out = []
    for h in range(H):
        p = level_p[0][h]
        for li, m in enumerate(same):
            p = jnp.where(m, level_p[li + 1][h], p)
        p = p + jnp.concatenate(diag_p[h], axis=0)
        out.append(jnp.dot(p.astype(BF16), head(v_bf, h), preferred_element_type=F32) + inter[h])
    o_ref[0] = jnp.concatenate(out, axis=1)

    @pl.when(pl.program_id(1) == pl.num_programs(1) - 1)
    def _():
        sf_ref[0] = st_ref[...]


def gla_scan(proj, lb_raw, s0, *, direction, first, layer):
    bsz, seq, _ = proj.shape
    T, H = HG_CHUNK, HG_HEADS
    nc = seq // T
    rev = direction == 1
    chunk = (lambda c: nc - 1 - c) if rev else (lambda c: c)
    o, sf = pl.pallas_call(
        functools.partial(_gla_kernel, rev=rev, first=first, layer=layer),
        grid=(bsz, nc),
        in_specs=[pl.BlockSpec((1, T, HG_INNER), lambda b, c: (b, chunk(c), COL_Q)),
                  pl.BlockSpec((1, T, HG_INNER), lambda b, c: (b, chunk(c), COL_F + direction)),
                  pl.BlockSpec((1, T, HG_INNER), lambda b, c: (b, chunk(c), COL_V)),
                  pl.BlockSpec(lb_raw.shape, lambda b, c: (0, 0)),
                  pl.BlockSpec((1, H, HG_DV, HG_DK), lambda b, c: (b, 0, 0, 0))],
        out_specs=[pl.BlockSpec((1, T, HG_INNER), lambda b, c: (b, chunk(c), 0)),
                   pl.BlockSpec((1, H, HG_DV, HG_DK), lambda b, c: (b, 0, 0, 0))],
        out_shape=[jax.ShapeDtypeStruct((bsz, seq, HG_INNER), F32),
                   jax.ShapeDtypeStruct((bsz, H, HG_DV, HG_DK), F32)],
        scratch_shapes=[pltpu.VMEM((H, HG_DV, HG_DK), F32),
                        pltpu.VMEM((T, HG_INNER), F32)],
        compiler_params=_params("parallel", "arbitrary"),
        name=f"gla_scan_d{direction}",
    )(proj, proj, proj, lb_raw, s0)
    return o, sf


def _mixout_kernel(yf_ref, yb_ref, z_ref, of_ref, ob_ref, g_ref, nws_ref, nwh_ref, w_ref,
                   x_ref, gate_ref, lg_ref, lb_ref, o_ref):
    def group_rms(t, groups):
        width = t.shape[-1] // groups
        parts = []
        for gi in range(groups):
            tg = t[:, gi * width:(gi + 1) * width]
            ms = jnp.mean(tg * tg, axis=-1, keepdims=True)
            parts.append(tg * lax.rsqrt(ms + EPS))
        return jnp.concatenate(parts, axis=1)

    ys = group_rms((yf_ref[0] + yb_ref[0]) * _silu(z_ref[0]), SSD_GROUPS) * nws_ref[...]
    os_ = group_rms(of_ref[0] + ob_ref[0], HG_HEADS) * nwh_ref[...] * _silu(g_ref[0])
    a = jnp.concatenate([ys, os_], axis=1).astype(BF16)
    y = jnp.dot(a, w_ref[...], preferred_element_type=F32)
    o_ref[0] = _res_ln(x_ref[0], gate_ref[0], y, lg_ref[...], lb_ref[...])


def mix_out(y_f, y_b, o_f, o_b, proj, ssd_nw, hg_nw, w_out, x, modl, ctx, ln_g, ln_b):
    bsz, seq, d = x.shape
    tm = min(seq, 256)
    row = lambda b, i: (b, i, 0)
    const = lambda b, i: (0, 0)
    act = pl.BlockSpec((1, tm, d), row)
    return pl.pallas_call(
        _mixout_kernel,
        grid=(bsz, seq // tm),
        in_specs=[act, act, pl.BlockSpec((1, tm, d), lambda b, i: (b, i, COL_Z)), act, act,
                  pl.BlockSpec((1, tm, d), lambda b, i: (b, i, COL_G)),
                  pl.BlockSpec((1, d), const), pl.BlockSpec((1, d), const),
                  pl.BlockSpec(w_out.shape, const), act, _mod_spec(2, ctx),
                  pl.BlockSpec((1, d), const), pl.BlockSpec((1, d), const)],
        out_specs=act,
        out_shape=jax.ShapeDtypeStruct((bsz, seq, d), F32),
        compiler_params=_params("parallel", "parallel"),
        name="mix_out",
    )(y_f, y_b, proj, o_f, o_b, proj, ssd_nw, hg_nw, w_out, x, modl, ln_g, ln_b)


def _ffn_conv_kernel(v_ref, g_ref, w_ref, b_ref, o_ref, gp_ref, c0_ref, c2_ref, *, rows, cols, rb):
    seq = rows * cols
    pad = SUBLANES
    margin = pad + (cols if rows > 1 else 0)
    width = g_ref.shape[-1]
    gp_ref[0:margin] = jnp.zeros((margin, width), F32)
    gp_ref[margin + seq:margin + seq + margin] = jnp.zeros((margin, width), F32)
    gp_ref[margin:margin + seq] = g_ref[0]
    n = rb + 2 * pad
    for t0 in range(0, seq, rb):
        e0 = margin + t0 - pad
        mid = gp_ref[e0:e0 + n]
        col_sums = []
        for j in range(3):
            c = w_ref[3 + j:4 + j, :] * mid
            if rows > 1:
                c = c + w_ref[j:j + 1, :] * gp_ref[e0 - cols:e0 - cols + n]
                c = c + w_ref[6 + j:7 + j, :] * gp_ref[e0 + cols:e0 + cols + n]
            col_sums.append(c)
        c0_ref[...] = col_sums[0]
        c2_ref[...] = col_sums[2]
        col = (lax.broadcasted_iota(jnp.int32, (rb, 1), 0) + t0) & (cols - 1)
        u = (col_sums[1][pad:pad + rb] + b_ref[...]
             + jnp.where(col > 0, c0_ref[pad - 1:pad - 1 + rb], 0.0)
             + jnp.where(col < cols - 1, c2_ref[pad + 1:pad + 1 + rb], 0.0))
        gelu = 0.5 * u * (1.0 + lax.erf(u * (2.0 ** -0.5)))
        o_ref[0, t0:t0 + rb] = (gelu * v_ref[0, t0:t0 + rb]).astype(o_ref.dtype)


def ffn_conv(vg, conv_w, conv_b, rows, cols):
    bsz, seq, _ = vg.shape
    assert seq == rows * cols and cols & (cols - 1) == 0
    tc = 256
    nj = D_FF // tc
    rb = min(seq, 256)
    margin = SUBLANES + (cols if rows > 1 else 0)
    return pl.pallas_call(
        functools.partial(_ffn_conv_kernel, rows=rows, cols=cols, rb=rb),
        grid=(bsz, nj),
        in_specs=[pl.BlockSpec((1, seq, tc), lambda b, j: (b, 0, j)),
                  pl.BlockSpec((1, seq, tc), lambda b, j: (b, 0, j + nj)),
                  pl.BlockSpec((9, tc), lambda b, j: (0, j)),
                  pl.BlockSpec((1, tc), lambda b, j: (0, j))],
        out_specs=pl.BlockSpec((1, seq, tc), lambda b, j: (b, 0, j)),
        out_shape=jax.ShapeDtypeStruct((bsz, seq, D_FF), BF16),
        scratch_shapes=[pltpu.VMEM((seq + 2 * margin, tc), F32),
                        pltpu.VMEM((rb + 2 * SUBLANES, tc), F32),
                        pltpu.VMEM((rb + 2 * SUBLANES, tc), F32)],
        compiler_params=_params("parallel", "parallel"),
        name="ffn_conv",
    )(vg, vg, conv_w, conv_b)


def _ffn_block(x, modl, ctx, w_up, conv_w, conv_b, w_down, ln_g, ln_b, rows, cols):
    vg = mod_matmul(x, modl, ctx, 3, 4, w_up, n_out=2 * D_FF, name="ffn_up")
    act = ffn_conv(vg, conv_w, conv_b, rows, cols)
    return matmul_res_ln(act, w_down, None, x, modl, ctx, 5, ln_g, ln_b, name="ffn_down")


def _in_proj(x, modl, ctx, w_main, w_dt, conv_w, conv_b):
    proj, dt = mod_matmul(x, modl, ctx, 0, 1, w_main, n_out=IN_MAIN, w_extra=w_dt, name="in_proj")
    xbc = xbc_conv(proj, conv_w, conv_b)
    dt_t = jnp.swapaxes(dt[:, :, :2 * SSD_HEADS], 1, 2)
    return proj, xbc, dt, dt_t


def kernel(x, c, ctx, c_ctx, ada_w, ada_b, post_ln_g, post_ln_b, mix_w_in, ssd_conv_w, ssd_conv_b, ssd_dt_bias,
           ssd_a_log, ssd_d, ssd_norm_w, hg_lb_raw, hg_norm_w, mix_w_out, conf_w1, conf_b1, conf_dw_w, conf_dw_b,
           conf_ln_g, conf_ln_b, conf_w2, conf_b2, ffn_w_up, ffn_conv_w, ffn_conv_b, ffn_w_down):
    bsz, seq, d = x.shape
    rows = seq // GRID_W
    ctx_len = ctx.shape[1]
    assert d == D_MODEL and bsz <= MOD_ROWS // 2

    cond = jnp.zeros((MOD_ROWS, d), F32).at[:bsz].set(c).at[MOD_ROWS // 2].set(c_ctx)
    mod = ada_table(cond, ada_w, ada_b).reshape(DEPTH, MOD_ROWS * N_MOD, 1, d)

    xc = ctx
    for l in range(DEPTH):
        modl = mod[l]
        ctx_next = any(j % 2 == 0 for j in range(l + 1, DEPTH))
        ln_g0, ln_b0 = post_ln_g[l, 0][None], post_ln_b[l, 0][None]
        ln_g1, ln_b1 = post_ln_g[l, 1][None], post_ln_b[l, 1][None]
        if l % 2 == 0:
            e = l // 2
            w_in = mix_w_in[e]
            o_z, o_xbc, o_dt = 0, SSD_INNER, SSD_INNER + SSD_XBC
            o_q = o_dt + 2 * SSD_HEADS
            w_main = jnp.concatenate([w_in[:, o_z:o_xbc], w_in[:, o_q:], w_in[:, o_xbc:o_dt]], axis=1).astype(BF16)
            w_dt = jnp.pad(w_in[:, o_dt:o_q], ((0, 0), (0, DT_PAD - 2 * SSD_HEADS))).astype(BF16)
            conv_w, conv_b = ssd_conv_w[e], ssd_conv_b[e][None]
            d_skip_x = jnp.repeat(ssd_d[e], SSD_HEAD_DIM)[None]
            w_out = mix_w_out[e].astype(BF16)
            lat = _in_proj(x, modl, False, w_main, w_dt, conv_w, conv_b)
            cx = _in_proj(xc, modl, True, w_main, w_dt, conv_w, conv_b)
            ys, os_ = {}, {}
            for dr in range(2):
                scan_args = dict(direction=dr, add_skip=dr == 0)
                zeros_ssd = jnp.zeros((bsz, SSD_STATE, SSD_INNER), F32)
                ys["c", dr], s_ctx = ssd_scan(cx[1], cx[2], cx[3], ssd_dt_bias[e, dr], ssd_a_log[e, dr], d_skip_x,
                                              zeros_ssd, **scan_args)
                ys["l", dr], _ = ssd_scan(lat[1], lat[2], lat[3], ssd_dt_bias[e, dr], ssd_a_log[e, dr], d_skip_x,
                                          s_ctx, **scan_args)
                zeros_hg = jnp.zeros((bsz, HG_HEADS, HG_DV, HG_DK), F32)
                os_["c", dr], st_ctx = gla_scan(cx[0], hg_lb_raw, zeros_hg, direction=dr, first=e == 0, layer=e)
                os_["l", dr], _ = gla_scan(lat[0], hg_lb_raw, st_ctx, direction=dr, first=e == 0, layer=e)
            nws, nwh = ssd_norm_w[e][None], hg_norm_w[e][None]
            x1 = mix_out(ys["l", 0], ys["l", 1], os_["l", 0], os_["l", 1], lat[0], nws, nwh, w_out,
                         x, modl, False, ln_g0, ln_b0)
            if ctx_next:
                xc1 = mix_out(ys["c", 0], ys["c", 1], os_["c", 0], os_["c", 1], cx[0], nws, nwh, w_out,
                              xc, modl, True, ln_g0, ln_b0)
        else:
            o = l // 2
            w1 = conf_w1[o].astype(BF16)
            w2 = conf_w2[o].astype(BF16)
            conv_args = (conf_dw_w[o], conf_dw_b[o][None], conf_ln_g[o][None], conf_ln_b[o][None])

            def conformer(t, is_ctx):
                glu = mod_matmul(t, modl, is_ctx, 0, 1, w1, n_out=d, glu=True, bias=conf_b1[o][None], name="conf_glu")
                u = conf_conv(glu, *conv_args)
                return matmul_res_ln(u, w2, conf_b2[o][None], t, modl, is_ctx, 2, ln_g0, ln_b0, name="conf_out")

            x1 = conformer(x, False)
            if ctx_next:
                xc1 = conformer(xc, True)
        ffn_args = (ffn_w_up[l].astype(BF16), ffn_conv_w[l].reshape(9, D_FF), ffn_conv_b[l][None],
                    ffn_w_down[l].astype(BF16), ln_g1, ln_b1)
        x = _ffn_block(x1, modl, False, *ffn_args, rows, GRID_W)
        if ctx_next:
            xc = _ffn_block(xc1, modl, True, *ffn_args, 1, ctx_len)
    return x
```

```python
import functools

import jax
import jax.numpy as jnp
from jax import lax
from jax.experimental import pallas as pl
from jax.experimental.pallas import tpu as pltpu

F32 = jnp.float32
BF16 = jnp.bfloat16
HIGHEST = lax.Precision.HIGHEST

D_MODEL = 1024
DEPTH = 4
GRID_W = 64
SSD_HEADS = 16
SSD_HEAD_DIM = 64
SSD_INNER = SSD_HEADS * SSD_HEAD_DIM
SSD_GROUPS = 2
SSD_STATE = 128
SSD_BC = SSD_GROUPS * SSD_STATE
SSD_XBC = SSD_INNER + 2 * SSD_BC
SSD_CHUNK = 128
HG_HEADS = 8
HG_DK = 128
HG_DV = D_MODEL // HG_HEADS
HG_INNER = HG_HEADS * HG_DK
HG_CHUNK = 128
HG_SUB = 8
D_FF = 2816
CONF_KERNEL = 31
ALPHA = (2 * DEPTH) ** 0.25
EPS = 1e-5
LOG2E = 1.4426950408889634
N_MOD = 6
MOD_ROWS = 8
DT_PAD = 128
LANES = 128
SUBLANES = 8
VMEM_LIMIT = 48 * 1024 * 1024
ACT = BF16
PACKED_ROWS = 16
MAX_TN = 1536

COL_Z, COL_Q, COL_F, COL_V, COL_G = 0, 1, 2, 4, 5
COL_XBC = 6 * D_MODEL
IN_MAIN = COL_XBC + SSD_XBC


def _params(*sem):
    return pltpu.CompilerParams(dimension_semantics=sem, vmem_limit_bytes=VMEM_LIMIT)


def _silu(t):
    return t * jax.nn.sigmoid(t)


def _softplus(t):
    return jnp.maximum(t, 0.0) + jnp.log1p(jnp.exp(-jnp.abs(t)))


def _split_dot(t, m, passes):
    acc, rem = None, t
    for i in range(passes):
        piece = rem.astype(BF16)
        part = jnp.dot(piece, m, preferred_element_type=F32)
        acc = part if acc is None else acc + part
        if i + 1 < passes:
            rem = rem - piece.astype(F32)
    return acc


def _res_ln(x, gate, y, g, b):
    r = ALPHA * x + gate * y
    mu = jnp.mean(r, axis=-1, keepdims=True)
    rc = r - mu
    var = jnp.mean(rc * rc, axis=-1, keepdims=True)
    return rc * lax.rsqrt(var + EPS) * g + b


def _mod_spec(k, ctx):
    if ctx:
        return pl.BlockSpec((1, 1, D_MODEL), lambda b, *_: ((MOD_ROWS // 2) * N_MOD + k, 0, 0))
    return pl.BlockSpec((1, 1, D_MODEL), lambda b, *_: (b * N_MOD + k, 0, 0))


def _ada_kernel(s_ref, w_ref, b_ref, o_ref):
    s = _silu(s_ref[...])
    o_ref[0] = jnp.dot(s, w_ref[0], precision=HIGHEST, preferred_element_type=F32) + b_ref[0]


def ada_table(s, ada_w, ada_b):
    depth, d, n = ada_w.shape
    tn = n // 4
    return pl.pallas_call(
        _ada_kernel,
        grid=(depth, n // tn),
        in_specs=[pl.BlockSpec((MOD_ROWS, d), lambda l, j: (0, 0)),
                  pl.BlockSpec((1, d, tn), lambda l, j: (l, 0, j)),
                  pl.BlockSpec((1, 1, tn), lambda l, j: (l, 0, j))],
        out_specs=pl.BlockSpec((1, MOD_ROWS, tn), lambda l, j: (l, 0, j)),
        out_shape=jax.ShapeDtypeStruct((depth, MOD_ROWS, n), F32),
        compiler_params=_params("parallel", "parallel"),
        name="ada_table",
    )(s, ada_w, ada_b.reshape(depth, 1, n))


def _mm_kernel(*refs, glu, has_bias, has_extra):
    it = iter(refs)
    x_ref, sh_ref, sc_ref, w_ref = next(it), next(it), next(it), next(it)
    wg_ref = next(it) if glu else None
    b_ref = next(it) if has_bias else None
    bg_ref = next(it) if (has_bias and glu) else None
    wx_ref = next(it) if has_extra else None
    o_ref = next(it)
    ox_ref = next(it) if has_extra else None
    xm_ref = next(it)

    @pl.when(pl.program_id(2) == 0)
    def _():
        xm = x_ref[0] * (1.0 + sc_ref[0]) + sh_ref[0]
        xm_ref[...] = xm.astype(BF16)
        if has_extra:
            ox_ref[0] = jnp.dot(xm_ref[...], wx_ref[...], preferred_element_type=F32)

    xm = xm_ref[...]
    acc = jnp.dot(xm, w_ref[...], preferred_element_type=F32)
    if has_bias:
        acc = acc + b_ref[...]
    if glu:
        gate = jnp.dot(xm, wg_ref[...], preferred_element_type=F32)
        if has_bias:
            gate = gate + bg_ref[...]
        acc = acc * jax.nn.sigmoid(gate)
    o_ref[0] = acc.astype(o_ref.dtype)


def mod_matmul(x, modl, ctx, k_shift, k_scale, w, *, n_out, glu=False, bias=None, w_extra=None, name):
    bsz, seq, d = x.shape
    tm = min(seq, 1024)
    tn = max(t for t in range(LANES, MAX_TN + 1, LANES) if n_out % t == 0)
    nj = n_out // tn
    in_specs = [pl.BlockSpec((1, tm, d), lambda b, i, j: (b, i, 0)),
                _mod_spec(k_shift, ctx), _mod_spec(k_scale, ctx),
                pl.BlockSpec((d, tn), lambda b, i, j: (0, j))]
    args = [x, modl, modl, w]
    if glu:
        in_specs.append(pl.BlockSpec((d, tn), lambda b, i, j: (0, j + nj)))
        args.append(w)
    if bias is not None:
        in_specs.append(pl.BlockSpec((1, tn), lambda b, i, j: (0, j)))
        args.append(bias)
        if glu:
            in_specs.append(pl.BlockSpec((1, tn), lambda b, i, j: (0, j + nj)))
            args.append(bias)
    out_shape = [jax.ShapeDtypeStruct((bsz, seq, n_out), ACT)]
    out_specs = [pl.BlockSpec((1, tm, tn), lambda b, i, j: (b, i, j))]
    if w_extra is not None:
        in_specs.append(pl.BlockSpec((d, DT_PAD), lambda b, i, j: (0, 0)))
        args.append(w_extra)
        out_shape.append(jax.ShapeDtypeStruct((bsz, seq, DT_PAD), F32))
        out_specs.append(pl.BlockSpec((1, tm, DT_PAD), lambda b, i, j: (b, i, 0)))
    res = pl.pallas_call(
        functools.partial(_mm_kernel, glu=glu, has_bias=bias is not None, has_extra=w_extra is not None),
        grid=(bsz, seq // tm, nj),
        in_specs=in_specs, out_specs=out_specs, out_shape=out_shape,
        scratch_shapes=[pltpu.VMEM((tm, d), BF16)],
        compiler_params=_params("parallel", "parallel", "arbitrary"),
        name=name,
    )(*args)
    return res if w_extra is not None else res[0]


def _resln_kernel(*refs, has_bias):
    it = iter(refs)
    a_ref, w_ref = next(it), next(it)
    b_ref = next(it) if has_bias else None
    x_ref, gate_ref, g_ref, beta_ref, o_ref = next(it), next(it), next(it), next(it), next(it)
    y = jnp.dot(a_ref[0], w_ref[...], preferred_element_type=F32)
    if has_bias:
        y = y + b_ref[...]
    o_ref[0] = _res_ln(x_ref[0], gate_ref[0], y, g_ref[...], beta_ref[...])


def matmul_res_ln(a, w, bias, x, modl, ctx, k_gate, ln_g, ln_b, *, name):
    bsz, seq, k = a.shape
    d = x.shape[-1]
    tm = min(seq, 512)
    row = lambda b, i: (b, i, 0)
    const = lambda b, i: (0, 0)
    in_specs = [pl.BlockSpec((1, tm, k), row), pl.BlockSpec((k, d), const)]
    args = [a, w]
    if bias is not None:
        in_specs.append(pl.BlockSpec((1, d), const))
        args.append(bias)
    in_specs += [pl.BlockSpec((1, tm, d), row), _mod_spec(k_gate, ctx),
                 pl.BlockSpec((1, d), const), pl.BlockSpec((1, d), const)]
    args += [x, modl, ln_g, ln_b]
    return pl.pallas_call(
        functools.partial(_resln_kernel, has_bias=bias is not None),
        grid=(bsz, seq // tm),
        in_specs=in_specs,
        out_specs=pl.BlockSpec((1, tm, d), row),
        out_shape=jax.ShapeDtypeStruct((bsz, seq, d), F32),
        compiler_params=_params("parallel", "parallel"),
        name=name,
    )(*args)


def _fill_halo(buf_ref, cur_ref, prev_ref, next_ref, halo, tl):
    i = pl.program_id(1)
    last = pl.num_programs(1) - 1
    buf_ref[0:halo] = jnp.where(i > 0, prev_ref[0].astype(F32), 0.0)
    buf_ref[halo:halo + tl] = cur_ref[0].astype(F32)
    buf_ref[halo + tl:halo + tl + halo] = jnp.where(i < last, next_ref[0].astype(F32), 0.0)


def _halo_specs(seq, tl, halo, width, col_block):
    per = tl // halo
    last = seq // halo - 1
    def prev_map(b, i, *rest):
        return (b, jnp.maximum(i * per - 1, 0), col_block(*rest))
    def next_map(b, i, *rest):
        return (b, jnp.minimum((i + 1) * per, last), col_block(*rest))
    return pl.BlockSpec((1, halo, width), prev_map), pl.BlockSpec((1, halo, width), next_map)


def _xbc_conv_kernel(cur_ref, prev_ref, next_ref, w_ref, b_ref, o_ref, buf_ref, *, taps, halo, tl):
    _fill_halo(buf_ref, cur_ref, prev_ref, next_ref, halo, tl)
    acc = jnp.broadcast_to(b_ref[...], (tl, b_ref.shape[-1]))
    for k in range(taps):
        off = halo - taps // 2 + k
        acc = acc + w_ref[k:k + 1, :] * buf_ref[off:off + tl, :]
    o_ref[0] = _silu(acc).astype(o_ref.dtype)


def xbc_conv(proj, conv_w, conv_b):
    bsz, seq, _ = proj.shape
    taps, width = conv_w.shape
    tc = 512
    tl = min(seq, 512)
    halo = PACKED_ROWS
    first = COL_XBC // tc
    prev_spec, next_spec = _halo_specs(seq, tl, halo, tc, lambda j: first + j)
    return pl.pallas_call(
        functools.partial(_xbc_conv_kernel, taps=taps, halo=halo, tl=tl),
        grid=(bsz, seq // tl, width // tc),
        in_specs=[pl.BlockSpec((1, tl, tc), lambda b, i, j: (b, i, first + j)), prev_spec, next_spec,
                  pl.BlockSpec((taps, tc), lambda b, i, j: (0, j)),
                  pl.BlockSpec((1, tc), lambda b, i, j: (0, j))],
        out_specs=pl.BlockSpec((1, tl, tc), lambda b, i, j: (b, i, j)),
        out_shape=jax.ShapeDtypeStruct((bsz, seq, width), ACT),
        scratch_shapes=[pltpu.VMEM((tl + 2 * halo, tc), F32)],
        compiler_params=_params("parallel", "parallel", "parallel"),
        name="xbc_conv",
    )(proj, proj, proj, conv_w, conv_b)


def _conf_conv_kernel(cur_ref, prev_ref, next_ref, w_ref, b_ref, g_ref, beta_ref, o_ref, buf_ref, ph_ref,
                      *, taps, halo, tl):
    _fill_halo(buf_ref, cur_ref, prev_ref, next_ref, halo, tl)
    rows = tl + 2 * halo
    full = buf_ref[...]
    for p in range(1, SUBLANES):
        ph_ref[p - 1] = pltpu.roll(full, rows - p, axis=0)
    acc = jnp.broadcast_to(b_ref[...], (tl, b_ref.shape[-1]))
    for k in range(taps):
        off = halo - taps // 2 + k
        p = off % SUBLANES
        base = off - p
        win = buf_ref[base:base + tl, :] if p == 0 else ph_ref[p - 1, base:base + tl, :]
        acc = acc + w_ref[k:k + 1, :] * win
    mu = jnp.mean(acc, axis=-1, keepdims=True)
    ac = acc - mu
    var = jnp.mean(ac * ac, axis=-1, keepdims=True)
    u = ac * lax.rsqrt(var + EPS) * g_ref[...] + beta_ref[...]
    o_ref[0] = _silu(u).astype(o_ref.dtype)


def conf_conv(glu, dw_w, dw_b, ln_g, ln_b):
    bsz, seq, d = glu.shape
    taps = dw_w.shape[0]
    tl = min(seq, 128)
    halo = PACKED_ROWS
    prev_spec, next_spec = _halo_specs(seq, tl, halo, d, lambda: 0)
    const = lambda b, i: (0, 0)
    return pl.pallas_call(
        functools.partial(_conf_conv_kernel, taps=taps, halo=halo, tl=tl),
        grid=(bsz, seq // tl),
        in_specs=[pl.BlockSpec((1, tl, d), lambda b, i: (b, i, 0)), prev_spec, next_spec,
                  pl.BlockSpec((taps, d), const), pl.BlockSpec((1, d), const),
                  pl.BlockSpec((1, d), const), pl.BlockSpec((1, d), const)],
        out_specs=pl.BlockSpec((1, tl, d), lambda b, i: (b, i, 0)),
        out_shape=jax.ShapeDtypeStruct((bsz, seq, d), BF16),
        scratch_shapes=[pltpu.VMEM((tl + 2 * halo, d), F32),
                        pltpu.VMEM((SUBLANES - 1, tl + 2 * halo, d), F32)],
        compiler_params=_params("parallel", "parallel"),
        name="conf_conv",
    )(glu, glu, glu, dw_w, dw_b, ln_g, ln_b)


def _ssd_kernel(xs_ref, b_ref, c_ref, dt_ref, dtt_ref, dtb_ref, dtbt_ref, alog_ref, alogt_ref, dsk_ref, s0_ref,
                y_ref, sf_ref, st_ref, *, rev, direction, add_skip):
    T = SSD_CHUNK
    H, P, N = SSD_HEADS, SSD_HEAD_DIM, SSD_STATE
    GW = SSD_INNER // SSD_GROUPS

    @pl.when(pl.program_id(1) == 0)
    def _():
        st_ref[...] = s0_ref[0]

    dt = _softplus(dt_ref[0][:, direction * H:(direction + 1) * H] + dtb_ref[...])
    a_col = dt * (-jnp.exp(alog_ref[...]))
    a_row = _softplus(dtt_ref[0] + dtbt_ref[...]) * (-jnp.exp(alogt_ref[...]))

    ti = lax.broadcasted_iota(jnp.int32, (T, T), 0)
    si = lax.broadcasted_iota(jnp.int32, (T, T), 1)
    valid = (si >= ti) if rev else (si <= ti)
    valid_f = valid.astype(F32)
    valid_t = ((ti >= si) if rev else (ti <= si)).astype(F32)
    acum = jnp.dot(valid_f, a_col, precision=HIGHEST, preferred_element_type=F32)
    acum_row = jnp.dot(a_row, valid_t, precision=HIGHEST, preferred_element_type=F32)
    atot = acum[0:1] if rev else acum[T - 1:T]

    e_p = (lax.broadcasted_iota(jnp.int32, (H, H * P), 1) // P
           == lax.broadcasted_iota(jnp.int32, (H, H * P), 0)).astype(BF16)
    expand = lambda t, passes: _split_dot(t, e_p, passes)
    dt_x = expand(dt, 2)
    eac_x = expand(jnp.exp(acum), 3)
    dec_x = expand(jnp.exp(atot - acum), 2)
    eat_x = expand(jnp.broadcast_to(jnp.exp(atot), (SUBLANES, H)), 3)[0:1]

    xs = xs_ref[0].astype(F32)
    xdt = xs * dt_x
    xdec = (xdt * dec_x).astype(BF16)
    bm = b_ref[0].astype(BF16)
    cm = c_ref[0].astype(BF16)
    lane = lax.broadcasted_iota(jnp.int32, (T, 2 * P), 1)

    y_parts = []
    for g in range(SSD_GROUPS):
        bg = bm[:, g * N:(g + 1) * N]
        cg = cm[:, g * N:(g + 1) * N]
        cb = lax.dot_general(cg, bg, (((1,), (1,)), ((), ())), preferred_element_type=F32)
        s_g = st_ref[:, g * GW:(g + 1) * GW]
        y_off = jnp.dot(cg, s_g.astype(BF16), preferred_element_type=F32) * eac_x[:, g * GW:(g + 1) * GW]
        upd = lax.dot_general(bg, xdec[:, g * GW:(g + 1) * GW], (((0,), (0,)), ((), ())),
                              preferred_element_type=F32)
        st_ref[:, g * GW:(g + 1) * GW] = s_g * eat_x[:, g * GW:(g + 1) * GW] + upd
        heads_per_group = H // SSD_GROUPS
        pair_parts = []
        for pp in range(heads_per_group // 2):
            l_pair = []
            for h in (g * heads_per_group + 2 * pp, g * heads_per_group + 2 * pp + 1):
                diff = acum[:, h:h + 1] - acum_row[h:h + 1, :]
                l_pair.append(jnp.where(valid, jnp.exp(jnp.minimum(diff, 0.0)), 0.0) * cb)
            lhs = jnp.concatenate(l_pair, axis=1).astype(BF16)
            col0 = (g * heads_per_group + 2 * pp) * P
            xp = xdt[:, col0:col0 + 2 * P]
            rhs = jnp.concatenate([jnp.where(lane < P, xp, 0.0), jnp.where(lane >= P, xp, 0.0)],
                                  axis=0).astype(BF16)
            pair_parts.append(jnp.dot(lhs, rhs, preferred_element_type=F32))
        y_parts.append(jnp.concatenate(pair_parts, axis=1) + y_off)
    y = jnp.concatenate(y_parts, axis=1)
    if add_skip:
        y = y + dsk_ref[...] * xs
    y_ref[0] = y.astype(y_ref.dtype)

    @pl.when(pl.program_id(1) == pl.num_programs(1) - 1)
    def _():
        sf_ref[0] = st_ref[...]


def ssd_scan(xbc, dt, dt_t, dt_bias, a_log, d_skip_x, s0, *, direction, add_skip):
    bsz, seq, _ = xbc.shape
    T, H = SSD_CHUNK, SSD_HEADS
    nc = seq // T
    rev = direction == 1
    chunk = (lambda c: nc - 1 - c) if rev else (lambda c: c)
    const = lambda b, c: (0, 0)
    y, sf = pl.pallas_call(
        functools.partial(_ssd_kernel, rev=rev, direction=direction, add_skip=add_skip),
        grid=(bsz, nc),
        in_specs=[pl.BlockSpec((1, T, SSD_INNER), lambda b, c: (b, chunk(c), 0)),
                  pl.BlockSpec((1, T, SSD_BC), lambda b, c: (b, chunk(c), SSD_INNER // SSD_BC)),
                  pl.BlockSpec((1, T, SSD_BC), lambda b, c: (b, chunk(c), SSD_INNER // SSD_BC + 1)),
                  pl.BlockSpec((1, T, DT_PAD), lambda b, c: (b, chunk(c), 0)),
                  pl.BlockSpec((1, H, T), lambda b, c: (b, direction, chunk(c))),
                  pl.BlockSpec((1, H), const), pl.BlockSpec((H, 1), const),
                  pl.BlockSpec((1, H), const), pl.BlockSpec((H, 1), const),
                  pl.BlockSpec((1, SSD_INNER), const),
                  pl.BlockSpec((1, SSD_STATE, SSD_INNER), lambda b, c: (b, 0, 0))],
        out_specs=[pl.BlockSpec((1, T, SSD_INNER), lambda b, c: (b, chunk(c), 0)),
                   pl.BlockSpec((1, SSD_STATE, SSD_INNER), lambda b, c: (b, 0, 0))],
        out_shape=[jax.ShapeDtypeStruct((bsz, seq, SSD_INNER), ACT),
                   jax.ShapeDtypeStruct((bsz, SSD_STATE, SSD_INNER), F32)],
        scratch_shapes=[pltpu.VMEM((SSD_STATE, SSD_INNER), F32)],
        compiler_params=_params("parallel", "arbitrary"),
        name=f"ssd_scan_d{direction}",
    )(xbc, xbc, xbc, dt, dt_t, dt_bias[None, :], dt_bias[:, None], a_log[None, :], a_log[:, None], d_skip_x, s0)
    return y, sf


def _gla_kernel(q_ref, f_ref, v_ref, lbraw_ref, s0_ref, o_ref, sf_ref, st_ref, c_scr, *, rev, first, layer):
    T, SB = HG_CHUNK, HG_SUB
    H, DK, DV = HG_HEADS, HG_DK, HG_DV
    W = H * DK
    NT = (((1,), (1,)), ((), ()))
    TN = (((0,), (0,)), ((), ()))

    @pl.when(pl.program_id(1) == 0)
    def _():
        st_ref[...] = s0_ref[0]

    a2 = f_ref[0].astype(F32) * LOG2E
    log_sig = jnp.minimum(a2, 0.0) - jnp.log(1.0 + jnp.exp2(-jnp.abs(a2))) * LOG2E
    log_sig_neg = log_sig - a2
    if first:
        lf, log_k = log_sig, log_sig_neg
    else:
        raw = lbraw_ref[...]
        pr = jnp.exp(raw - jnp.max(raw, axis=0, keepdims=True))
        pr = pr / jnp.sum(pr, axis=0, keepdims=True)
        lb = jnp.sum(pr[0:layer + 1], axis=0, keepdims=True) - pr[0:1]
        lf = jnp.log(lb + (1.0 - lb) * jnp.exp2(log_sig)) * LOG2E
        log_k = jnp.log(1.0 - lb) * LOG2E + log_sig_neg
    kk = jnp.exp2(log_k)
    q = _silu(q_ref[0].astype(F32))
    v_bf = v_ref[0].astype(BF16)

    ti = lax.broadcasted_iota(jnp.int32, (T, T), 0)
    si = lax.broadcasted_iota(jnp.int32, (T, T), 1)
    valid_f = ((si >= ti) if rev else (si <= ti)).astype(F32)
    acum = jnp.dot(valid_f, lf, precision=HIGHEST, preferred_element_type=F32)
    atot = acum[0:1] if rev else acum[T - 1:T]
    qe = (q * jnp.exp2(acum)).astype(BF16)
    kd = (kk * jnp.exp2(atot - acum)).astype(BF16)
    c_scr[...] = acum - log_k
    head = lambda t, h: t[:, h * DK:(h + 1) * DK]

    level_p = []
    b = T // 2
    while b >= SB:
        q_parts, k_parts = [], []
        zeros = jnp.zeros((b, W), F32)
        for c0 in range(0, T, 2 * b):
            lo, hi = slice(c0, c0 + b), slice(c0 + b, c0 + 2 * b)
            src, tgt = (hi, lo) if rev else (lo, hi)
            ref_row = c0 + b if rev else c0 + b - 1
            r = acum[ref_row:ref_row + 1]
            q_t = q[tgt] * jnp.exp2(acum[tgt] - r)
            k_s = kk[src] * jnp.exp2(r - acum[src])
            q_parts += [q_t, zeros] if rev else [zeros, q_t]
            k_parts += [zeros, k_s] if rev else [k_s, zeros]
        q_l = jnp.concatenate(q_parts, axis=0).astype(BF16)
        k_l = jnp.concatenate(k_parts, axis=0).astype(BF16)
        level_p.append([lax.dot_general(head(q_l, h), head(k_l, h), NT, preferred_element_type=F32)
                        for h in range(H)])
        b //= 2

    inter = [lax.dot_general(head(qe, h), st_ref[h].astype(BF16), NT, preferred_element_type=F32) for h in range(H)]
    upd = [lax.dot_general(head(v_bf, h), head(kd, h), TN, preferred_element_type=F32) for h in range(H)]
    for h in range(H):
        st_ref[h] = st_ref[h] * jnp.exp2(head(atot, h)) + upd[h]

    row = lax.broadcasted_iota(jnp.int32, (SB, 1), 0)
    lane = lax.broadcasted_iota(jnp.int32, (SB, T), 1)
    diag_p = [[] for _ in range(H)]
    for lo in range(0, T, SB):
        a_i, q_i = acum[lo:lo + SB], q[lo:lo + SB]
        p_h = [jnp.zeros((SB, T), F32) for _ in range(H)]
        for s in range(SB):
            c_s = jnp.broadcast_to(c_scr[lo + s:lo + s + 1, :], (SB, W))
            ok = (row <= s) if rev else (row >= s)
            w = q_i * jnp.exp2(jnp.where(ok, a_i - c_s, -1e30))
            here = lane == lo + s
            for h in range(H):
                p_h[h] = jnp.where(here, jnp.sum(head(w, h), axis=-1, keepdims=True), p_h[h])
        for h in range(H):
            diag_p[h].append(p_h[h])

    same = []
    b = T // 4
    while b >= SB:
        shift = (2 * b).bit_length() - 1
        same.append((ti >> shift) == (si >> shift))
        b //= 2
    out = []
    for h in range(H):
        p = level_p[0][h]
        for li, m in enumerate(same):
            p = jnp.where(m, level_p[li + 1][h], p)
        p = p + jnp.concatenate(diag_p[h], axis=0)
        out.append(jnp.dot(p.astype(BF16), head(v_bf, h), preferred_element_type=F32) + inter[h])
    o_ref[0] = jnp.concatenate(out, axis=1).astype(o_ref.dtype)

    @pl.when(pl.program_id(1) == pl.num_programs(1) - 1)
    def _():
        sf_ref[0] = st_ref[...]


def gla_scan(proj, lb_raw, s0, *, direction, first, layer):
    bsz, seq, _ = proj.shape
    T, H = HG_CHUNK, HG_HEADS
    nc = seq // T
    rev = direction == 1
    chunk = (lambda c: nc - 1 - c) if rev else (lambda c: c)
    o, sf = pl.pallas_call(
        functools.partial(_gla_kernel, rev=rev, first=first, layer=layer),
        grid=(bsz, nc),
        in_specs=[pl.BlockSpec((1, T, HG_INNER), lambda b, c: (b, chunk(c), COL_Q)),
                  pl.BlockSpec((1, T, HG_INNER), lambda b, c: (b, chunk(c), COL_F + direction)),
                  pl.BlockSpec((1, T, HG_INNER), lambda b, c: (b, chunk(c), COL_V)),
                  pl.BlockSpec(lb_raw.shape, lambda b, c: (0, 0)),
                  pl.BlockSpec((1, H, HG_DV, HG_DK), lambda b, c: (b, 0, 0, 0))],
        out_specs=[pl.BlockSpec((1, T, HG_INNER), lambda b, c: (b, chunk(c), 0)),
                   pl.BlockSpec((1, H, HG_DV, HG_DK), lambda b, c: (b, 0, 0, 0))],
        out_shape=[jax.ShapeDtypeStruct((bsz, seq, HG_INNER), ACT),
                   jax.ShapeDtypeStruct((bsz, H, HG_DV, HG_DK), F32)],
        scratch_shapes=[pltpu.VMEM((H, HG_DV, HG_DK), F32),
                        pltpu.VMEM((T, HG_INNER), F32)],
        compiler_params=_params("parallel", "arbitrary"),
        name=f"gla_scan_d{direction}",
    )(proj, proj, proj, lb_raw, s0)
    return o, sf


def _mixout_kernel(yf_ref, yb_ref, z_ref, of_ref, ob_ref, g_ref, nws_ref, nwh_ref, w_ref,
                   x_ref, gate_ref, lg_ref, lb_ref, o_ref):
    def group_rms(t, groups):
        width = t.shape[-1] // groups
        parts = []
        for gi in range(groups):
            tg = t[:, gi * width:(gi + 1) * width]
            ms = jnp.mean(tg * tg, axis=-1, keepdims=True)
            parts.append(tg * lax.rsqrt(ms + EPS))
        return jnp.concatenate(parts, axis=1)

    f32 = lambda r: r[0].astype(F32)
    ys = group_rms((f32(yf_ref) + f32(yb_ref)) * _silu(f32(z_ref)), SSD_GROUPS) * nws_ref[...]
    os_ = group_rms(f32(of_ref) + f32(ob_ref), HG_HEADS) * nwh_ref[...] * _silu(f32(g_ref))
    a = jnp.concatenate([ys, os_], axis=1).astype(BF16)
    y = jnp.dot(a, w_ref[...], preferred_element_type=F32)
    o_ref[0] = _res_ln(x_ref[0], gate_ref[0], y, lg_ref[...], lb_ref[...])


def mix_out(y_f, y_b, o_f, o_b, proj, ssd_nw, hg_nw, w_out, x, modl, ctx, ln_g, ln_b):
    bsz, seq, d = x.shape
    tm = min(seq, 256)
    row = lambda b, i: (b, i, 0)
    const = lambda b, i: (0, 0)
    act = pl.BlockSpec((1, tm, d), row)
    return pl.pallas_call(
        _mixout_kernel,
        grid=(bsz, seq // tm),
        in_specs=[act, act, pl.BlockSpec((1, tm, d), lambda b, i: (b, i, COL_Z)), act, act,
                  pl.BlockSpec((1, tm, d), lambda b, i: (b, i, COL_G)),
                  pl.BlockSpec((1, d), const), pl.BlockSpec((1, d), const),
                  pl.BlockSpec(w_out.shape, const), act, _mod_spec(2, ctx),
                  pl.BlockSpec((1, d), const), pl.BlockSpec((1, d), const)],
        out_specs=act,
        out_shape=jax.ShapeDtypeStruct((bsz, seq, d), F32),
        compiler_params=_params("parallel", "parallel"),
        name="mix_out",
    )(y_f, y_b, proj, o_f, o_b, proj, ssd_nw, hg_nw, w_out, x, modl, ln_g, ln_b)


def _ffn_conv_kernel(v_ref, g_ref, w_ref, b_ref, o_ref, gp_ref, c0_ref, c2_ref, *, rows, cols, rb):
    seq = rows * cols
    pad = SUBLANES
    margin = pad + (cols if rows > 1 else 0)
    width = g_ref.shape[-1]
    gp_ref[0:margin] = jnp.zeros((margin, width), F32)
    gp_ref[margin + seq:margin + seq + margin] = jnp.zeros((margin, width), F32)
    gp_ref[margin:margin + seq] = g_ref[0].astype(F32)
    n = rb + 2 * pad
    for t0 in range(0, seq, rb):
        e0 = margin + t0 - pad
        mid = gp_ref[e0:e0 + n]
        col_sums = []
        for j in range(3):
            c = w_ref[3 + j:4 + j, :] * mid
            if rows > 1:
                c = c + w_ref[j:j + 1, :] * gp_ref[e0 - cols:e0 - cols + n]
                c = c + w_ref[6 + j:7 + j, :] * gp_ref[e0 + cols:e0 + cols + n]
            col_sums.append(c)
        c0_ref[...] = col_sums[0]
        c2_ref[...] = col_sums[2]
        col = (lax.broadcasted_iota(jnp.int32, (rb, 1), 0) + t0) & (cols - 1)
        u = (col_sums[1][pad:pad + rb] + b_ref[...]
             + jnp.where(col > 0, c0_ref[pad - 1:pad - 1 + rb], 0.0)
             + jnp.where(col < cols - 1, c2_ref[pad + 1:pad + 1 + rb], 0.0))
        gelu = 0.5 * u * (1.0 + lax.erf(u * (2.0 ** -0.5)))
        o_ref[0, t0:t0 + rb] = (gelu * v_ref[0, t0:t0 + rb].astype(F32)).astype(o_ref.dtype)


def ffn_conv(vg, conv_w, conv_b, rows, cols):
    bsz, seq, _ = vg.shape
    assert seq == rows * cols and cols & (cols - 1) == 0
    tc = 256
    nj = D_FF // tc
    rb = min(seq, 256)
    margin = SUBLANES + (cols if rows > 1 else 0)
    return pl.pallas_call(
        functools.partial(_ffn_conv_kernel, rows=rows, cols=cols, rb=rb),
        grid=(bsz, nj),
        in_specs=[pl.BlockSpec((1, seq, tc), lambda b, j: (b, 0, j)),
                  pl.BlockSpec((1, seq, tc), lambda b, j: (b, 0, j + nj)),
                  pl.BlockSpec((9, tc), lambda b, j: (0, j)),
                  pl.BlockSpec((1, tc), lambda b, j: (0, j))],
        out_specs=pl.BlockSpec((1, seq, tc), lambda b, j: (b, 0, j)),
        out_shape=jax.ShapeDtypeStruct((bsz, seq, D_FF), BF16),
        scratch_shapes=[pltpu.VMEM((seq + 2 * margin, tc), F32),
                        pltpu.VMEM((rb + 2 * SUBLANES, tc), F32),
                        pltpu.VMEM((rb + 2 * SUBLANES, tc), F32)],
        compiler_params=_params("parallel", "parallel"),
        name="ffn_conv",
    )(vg, vg, conv_w, conv_b)


def _ffn_block(x, modl, ctx, w_up, conv_w, conv_b, w_down, ln_g, ln_b, rows, cols):
    vg = mod_matmul(x, modl, ctx, 3, 4, w_up, n_out=2 * D_FF, name="ffn_up")
    act = ffn_conv(vg, conv_w, conv_b, rows, cols)
    return matmul_res_ln(act, w_down, None, x, modl, ctx, 5, ln_g, ln_b, name="ffn_down")


def _in_proj(x, modl, ctx, w_main, w_dt, conv_w, conv_b):
    proj, dt = mod_matmul(x, modl, ctx, 0, 1, w_main, n_out=IN_MAIN, w_extra=w_dt, name="in_proj")
    xbc = xbc_conv(proj, conv_w, conv_b)
    dt_t = jnp.swapaxes(dt[:, :, :2 * SSD_HEADS], 1, 2)
    return proj, xbc, dt, dt_t


def kernel(x, c, ctx, c_ctx, ada_w, ada_b, post_ln_g, post_ln_b, mix_w_in, ssd_conv_w, ssd_conv_b, ssd_dt_bias,
           ssd_a_log, ssd_d, ssd_norm_w, hg_lb_raw, hg_norm_w, mix_w_out, conf_w1, conf_b1, conf_dw_w, conf_dw_b,
           conf_ln_g, conf_ln_b, conf_w2, conf_b2, ffn_w_up, ffn_conv_w, ffn_conv_b, ffn_w_down):
    bsz, seq, d = x.shape
    rows = seq // GRID_W
    ctx_len = ctx.shape[1]
    assert d == D_MODEL and bsz <= MOD_ROWS // 2

    cond = jnp.zeros((MOD_ROWS, d), F32).at[:bsz].set(c).at[MOD_ROWS // 2].set(c_ctx)
    mod = ada_table(cond, ada_w, ada_b).reshape(DEPTH, MOD_ROWS * N_MOD, 1, d)

    xc = ctx
    for l in range(DEPTH):
        modl = mod[l]
        ctx_next = any(j % 2 == 0 for j in range(l + 1, DEPTH))
        ln_g0, ln_b0 = post_ln_g[l, 0][None], post_ln_b[l, 0][None]
        ln_g1, ln_b1 = post_ln_g[l, 1][None], post_ln_b[l, 1][None]
        if l % 2 == 0:
            e = l // 2
            w_in = mix_w_in[e]
            o_z, o_xbc, o_dt = 0, SSD_INNER, SSD_INNER + SSD_XBC
            o_q = o_dt + 2 * SSD_HEADS
            w_main = jnp.concatenate([w_in[:, o_z:o_xbc], w_in[:, o_q:], w_in[:, o_xbc:o_dt]], axis=1).astype(BF16)
            w_dt = jnp.pad(w_in[:, o_dt:o_q], ((0, 0), (0, DT_PAD - 2 * SSD_HEADS))).astype(BF16)
            conv_w, conv_b = ssd_conv_w[e], ssd_conv_b[e][None]
            d_skip_x = jnp.repeat(ssd_d[e], SSD_HEAD_DIM)[None]
            w_out = mix_w_out[e].astype(BF16)
            lat = _in_proj(x, modl, False, w_main, w_dt, conv_w, conv_b)
            cx = _in_proj(xc, modl, True, w_main, w_dt, conv_w, conv_b)
            ys, os_ = {}, {}
            for dr in range(2):
                scan_args = dict(direction=dr, add_skip=dr == 0)
                zeros_ssd = jnp.zeros((bsz, SSD_STATE, SSD_INNER), F32)
                ys["c", dr], s_ctx = ssd_scan(cx[1], cx[2], cx[3], ssd_dt_bias[e, dr], ssd_a_log[e, dr], d_skip_x,
                                              zeros_ssd, **scan_args)
                ys["l", dr], _ = ssd_scan(lat[1], lat[2], lat[3], ssd_dt_bias[e, dr], ssd_a_log[e, dr], d_skip_x,
                                          s_ctx, **scan_args)
                zeros_hg = jnp.zeros((bsz, HG_HEADS, HG_DV, HG_DK), F32)
                os_["c", dr], st_ctx = gla_scan(cx[0], hg_lb_raw, zeros_hg, direction=dr, first=e == 0, layer=e)
                os_["l", dr], _ = gla_scan(lat[0], hg_lb_raw, st_ctx, direction=dr, first=e == 0, layer=e)
            nws, nwh = ssd_norm_w[e][None], hg_norm_w[e][None]
            x1 = mix_out(ys["l", 0], ys["l", 1], os_["l", 0], os_["l", 1], lat[0], nws, nwh, w_out,
                         x, modl, False, ln_g0, ln_b0)
            if ctx_next:
                xc1 = mix_out(ys["c", 0], ys["c", 1], os_["c", 0], os_["c", 1], cx[0], nws, nwh, w_out,
                              xc, modl, True, ln_g0, ln_b0)
        else:
            o = l // 2
            w1 = conf_w1[o].astype(BF16)
            w2 = conf_w2[o].astype(BF16)
            conv_args = (conf_dw_w[o], conf_dw_b[o][None], conf_ln_g[o][None], conf_ln_b[o][None])

            def conformer(t, is_ctx):
                glu = mod_matmul(t, modl, is_ctx, 0, 1, w1, n_out=d, glu=True, bias=conf_b1[o][None], name="conf_glu")
                u = conf_conv(glu, *conv_args)
                return matmul_res_ln(u, w2, conf_b2[o][None], t, modl, is_ctx, 2, ln_g0, ln_b0, name="conf_out")

            x1 = conformer(x, False)
            if ctx_next:
                xc1 = conformer(xc, True)
        ffn_args = (ffn_w_up[l].astype(BF16), ffn_conv_w[l].reshape(9, D_FF), ffn_conv_b[l][None],
                    ffn_w_down[l].astype(BF16), ln_g1, ln_b1)
        x = _ffn_block(x1, modl, False, *ffn_args, rows, GRID_W)
        if ctx_next:
            xc = _ffn_block(xc1, modl, True, *ffn_args, 1, ctx_len)
    return x
```

```python
import functools

import jax
import jax.numpy as jnp
from jax import lax
from jax.experimental import pallas as pl
from jax.experimental.pallas import tpu as pltpu

F32 = jnp.float32
BF16 = jnp.bfloat16
HIGHEST = lax.Precision.HIGHEST

D_MODEL = 1024
DEPTH = 4
GRID_W = 64
SSD_HEADS = 16
SSD_HEAD_DIM = 64
SSD_INNER = SSD_HEADS * SSD_HEAD_DIM
SSD_GROUPS = 2
SSD_STATE = 128
SSD_BC = SSD_GROUPS * SSD_STATE
SSD_XBC = SSD_INNER + 2 * SSD_BC
SSD_CHUNK = 128
HG_HEADS = 8
HG_DK = 128
HG_DV = D_MODEL // HG_HEADS
HG_INNER = HG_HEADS * HG_DK
HG_CHUNK = 128
HG_SUB = 8
D_FF = 2816
CONF_KERNEL = 31
ALPHA = (2 * DEPTH) ** 0.25
EPS = 1e-5
LOG2E = 1.4426950408889634
N_MOD = 6
MOD_ROWS = 8
DT_PAD = 128
LANES = 128
SUBLANES = 8
VMEM_LIMIT = 48 * 1024 * 1024
ACT = BF16
PACKED_ROWS = 16
MAX_TN = 1536

COL_Z, COL_Q, COL_F, COL_V, COL_G = 0, 1, 2, 4, 5
COL_XBC = 6 * D_MODEL
IN_MAIN = COL_XBC + SSD_XBC


def _params(*sem):
    return pltpu.CompilerParams(dimension_semantics=sem, vmem_limit_bytes=VMEM_LIMIT)


def _silu(t):
    return t * jax.nn.sigmoid(t)


def _softplus(t):
    return jnp.maximum(t, 0.0) + jnp.log1p(jnp.exp(-jnp.abs(t)))


def _split_dot(t, m, passes):
    acc, rem = None, t
    for i in range(passes):
        piece = rem.astype(BF16)
        part = jnp.dot(piece, m, preferred_element_type=F32)
        acc = part if acc is None else acc + part
        if i + 1 < passes:
            rem = rem - piece.astype(F32)
    return acc


def _res_ln(x, gate, y, g, b):
    r = ALPHA * x + gate * y
    mu = jnp.mean(r, axis=-1, keepdims=True)
    rc = r - mu
    var = jnp.mean(rc * rc, axis=-1, keepdims=True)
    return rc * lax.rsqrt(var + EPS) * g + b


def _mod_spec(k, ctx):
    if ctx:
        return pl.BlockSpec((1, 1, D_MODEL), lambda b, *_: ((MOD_ROWS // 2) * N_MOD + k, 0, 0))
    return pl.BlockSpec((1, 1, D_MODEL), lambda b, *_: (b * N_MOD + k, 0, 0))


def _ada_kernel(s_ref, w_ref, b_ref, o_ref):
    s = _silu(s_ref[...])
    o_ref[0] = jnp.dot(s, w_ref[0], precision=HIGHEST, preferred_element_type=F32) + b_ref[0]


def ada_table(s, ada_w, ada_b):
    depth, d, n = ada_w.shape
    tn = n // 4
    return pl.pallas_call(
        _ada_kernel,
        grid=(depth, n // tn),
        in_specs=[pl.BlockSpec((MOD_ROWS, d), lambda l, j: (0, 0)),
                  pl.BlockSpec((1, d, tn), lambda l, j: (l, 0, j)),
                  pl.BlockSpec((1, 1, tn), lambda l, j: (l, 0, j))],
        out_specs=pl.BlockSpec((1, MOD_ROWS, tn), lambda l, j: (l, 0, j)),
        out_shape=jax.ShapeDtypeStruct((depth, MOD_ROWS, n), F32),
        compiler_params=_params("parallel", "parallel"),
        name="ada_table",
    )(s, ada_w, ada_b.reshape(depth, 1, n))


def _mm_kernel(*refs, glu, has_bias, has_extra):
    it = iter(refs)
    x_ref, sh_ref, sc_ref, w_ref = next(it), next(it), next(it), next(it)
    wg_ref = next(it) if glu else None
    b_ref = next(it) if has_bias else None
    bg_ref = next(it) if (has_bias and glu) else None
    wx_ref = next(it) if has_extra else None
    o_ref = next(it)
    ox_ref = next(it) if has_extra else None
    xm_ref = next(it)

    @pl.when(pl.program_id(2) == 0)
    def _():
        xm = x_ref[0] * (1.0 + sc_ref[0]) + sh_ref[0]
        xm_ref[...] = xm.astype(BF16)
        if has_extra:
            ox_ref[0] = jnp.dot(xm_ref[...], wx_ref[...], preferred_element_type=F32)

    xm = xm_ref[...]
    acc = jnp.dot(xm, w_ref[...], preferred_element_type=F32)
    if has_bias:
        acc = acc + b_ref[...]
    if glu:
        gate = jnp.dot(xm, wg_ref[...], preferred_element_type=F32)
        if has_bias:
            gate = gate + bg_ref[...]
        acc = acc * jax.nn.sigmoid(gate)
    o_ref[0] = acc.astype(o_ref.dtype)


def mod_matmul(x, modl, ctx, k_shift, k_scale, w, *, n_out, glu=False, bias=None, w_extra=None, name):
    bsz, seq, d = x.shape
    tm = min(seq, 1024)
    tn = max(t for t in range(LANES, MAX_TN + 1, LANES) if n_out % t == 0)
    nj = n_out // tn
    in_specs = [pl.BlockSpec((1, tm, d), lambda b, i, j: (b, i, 0)),
                _mod_spec(k_shift, ctx), _mod_spec(k_scale, ctx),
                pl.BlockSpec((d, tn), lambda b, i, j: (0, j))]
    args = [x, modl, modl, w]
    if glu:
        in_specs.append(pl.BlockSpec((d, tn), lambda b, i, j: (0, j + nj)))
        args.append(w)
    if bias is not None:
        in_specs.append(pl.BlockSpec((1, tn), lambda b, i, j: (0, j)))
        args.append(bias)
        if glu:
            in_specs.append(pl.BlockSpec((1, tn), lambda b, i, j: (0, j + nj)))
            args.append(bias)
    out_shape = [jax.ShapeDtypeStruct((bsz, seq, n_out), ACT)]
    out_specs = [pl.BlockSpec((1, tm, tn), lambda b, i, j: (b, i, j))]
    if w_extra is not None:
        in_specs.append(pl.BlockSpec((d, DT_PAD), lambda b, i, j: (0, 0)))
        args.append(w_extra)
        out_shape.append(jax.ShapeDtypeStruct((bsz, seq, DT_PAD), F32))
        out_specs.append(pl.BlockSpec((1, tm, DT_PAD), lambda b, i, j: (b, i, 0)))
    res = pl.pallas_call(
        functools.partial(_mm_kernel, glu=glu, has_bias=bias is not None, has_extra=w_extra is not None),
        grid=(bsz, seq // tm, nj),
        in_specs=in_specs, out_specs=out_specs, out_shape=out_shape,
        scratch_shapes=[pltpu.VMEM((tm, d), BF16)],
        compiler_params=_params("parallel", "parallel", "arbitrary"),
        name=name,
    )(*args)
    return res if w_extra is not None else res[0]


def _resln_kernel(*refs, has_bias):
    it = iter(refs)
    a_ref, w_ref = next(it), next(it)
    b_ref = next(it) if has_bias else None
    x_ref, gate_ref, g_ref, beta_ref, o_ref = next(it), next(it), next(it), next(it), next(it)
    y = jnp.dot(a_ref[0], w_ref[...], preferred_element_type=F32)
    if has_bias:
        y = y + b_ref[...]
    o_ref[0] = _res_ln(x_ref[0], gate_ref[0], y, g_ref[...], beta_ref[...])


def matmul_res_ln(a, w, bias, x, modl, ctx, k_gate, ln_g, ln_b, *, name):
    bsz, seq, k = a.shape
    d = x.shape[-1]
    tm = min(seq, 512)
    row = lambda b, i: (b, i, 0)
    const = lambda b, i: (0, 0)
    in_specs = [pl.BlockSpec((1, tm, k), row), pl.BlockSpec((k, d), const)]
    args = [a, w]
    if bias is not None:
        in_specs.append(pl.BlockSpec((1, d), const))
        args.append(bias)
    in_specs += [pl.BlockSpec((1, tm, d), row), _mod_spec(k_gate, ctx),
                 pl.BlockSpec((1, d), const), pl.BlockSpec((1, d), const)]
    args += [x, modl, ln_g, ln_b]
    return pl.pallas_call(
        functools.partial(_resln_kernel, has_bias=bias is not None),
        grid=(bsz, seq // tm),
        in_specs=in_specs,
        out_specs=pl.BlockSpec((1, tm, d), row),
        out_shape=jax.ShapeDtypeStruct((bsz, seq, d), F32),
        compiler_params=_params("parallel", "parallel"),
        name=name,
    )(*args)


def _fill_halo(buf_ref, cur_ref, prev_ref, next_ref, halo, tl):
    i = pl.program_id(1)
    last = pl.num_programs(1) - 1
    buf_ref[0:halo] = jnp.where(i > 0, prev_ref[0].astype(F32), 0.0)
    buf_ref[halo:halo + tl] = cur_ref[0].astype(F32)
    buf_ref[halo + tl:halo + tl + halo] = jnp.where(i < last, next_ref[0].astype(F32), 0.0)


def _halo_specs(seq, tl, halo, width, col_block):
    per = tl // halo
    last = seq // halo - 1
    def prev_map(b, i, *rest):
        return (b, jnp.maximum(i * per - 1, 0), col_block(*rest))
    def next_map(b, i, *rest):
        return (b, jnp.minimum((i + 1) * per, last), col_block(*rest))
    return pl.BlockSpec((1, halo, width), prev_map), pl.BlockSpec((1, halo, width), next_map)


def _xbc_conv_kernel(cur_ref, prev_ref, next_ref, w_ref, b_ref, o_ref, buf_ref, *, taps, halo, tl):
    _fill_halo(buf_ref, cur_ref, prev_ref, next_ref, halo, tl)
    acc = jnp.broadcast_to(b_ref[...], (tl, b_ref.shape[-1]))
    for k in range(taps):
        off = halo - taps // 2 + k
        acc = acc + w_ref[k:k + 1, :] * buf_ref[off:off + tl, :]
    o_ref[0] = _silu(acc).astype(o_ref.dtype)


def xbc_conv(proj, conv_w, conv_b):
    bsz, seq, _ = proj.shape
    taps, width = conv_w.shape
    tc = 512
    tl = min(seq, 512)
    halo = PACKED_ROWS
    first = COL_XBC // tc
    prev_spec, next_spec = _halo_specs(seq, tl, halo, tc, lambda j: first + j)
    return pl.pallas_call(
        functools.partial(_xbc_conv_kernel, taps=taps, halo=halo, tl=tl),
        grid=(bsz, seq // tl, width // tc),
        in_specs=[pl.BlockSpec((1, tl, tc), lambda b, i, j: (b, i, first + j)), prev_spec, next_spec,
                  pl.BlockSpec((taps, tc), lambda b, i, j: (0, j)),
                  pl.BlockSpec((1, tc), lambda b, i, j: (0, j))],
        out_specs=pl.BlockSpec((1, tl, tc), lambda b, i, j: (b, i, j)),
        out_shape=jax.ShapeDtypeStruct((bsz, seq, width), ACT),
        scratch_shapes=[pltpu.VMEM((tl + 2 * halo, tc), F32)],
        compiler_params=_params("parallel", "parallel", "parallel"),
        name="xbc_conv",
    )(proj, proj, proj, conv_w, conv_b)


def _conf_conv_kernel(cur_ref, prev_ref, next_ref, w_ref, b_ref, g_ref, beta_ref, o_ref, buf_ref, ph_ref,
                      *, taps, halo, tl):
    _fill_halo(buf_ref, cur_ref, prev_ref, next_ref, halo, tl)
    rows = tl + 2 * halo
    full = buf_ref[...]
    for p in range(1, SUBLANES):
        ph_ref[p - 1] = pltpu.roll(full, rows - p, axis=0)
    acc = jnp.broadcast_to(b_ref[...], (tl, b_ref.shape[-1]))
    for k in range(taps):
        off = halo - taps // 2 + k
        p = off % SUBLANES
        base = off - p
        win = buf_ref[base:base + tl, :] if p == 0 else ph_ref[p - 1, base:base + tl, :]
        acc = acc + w_ref[k:k + 1, :] * win
    mu = jnp.mean(acc, axis=-1, keepdims=True)
    ac = acc - mu
    var = jnp.mean(ac * ac, axis=-1, keepdims=True)
    u = ac * lax.rsqrt(var + EPS) * g_ref[...] + beta_ref[...]
    o_ref[0] = _silu(u).astype(o_ref.dtype)


def conf_conv(glu, dw_w, dw_b, ln_g, ln_b):
    bsz, seq, d = glu.shape
    taps = dw_w.shape[0]
    tl = min(seq, 128)
    halo = PACKED_ROWS
    prev_spec, next_spec = _halo_specs(seq, tl, halo, d, lambda: 0)
    const = lambda b, i: (0, 0)
    return pl.pallas_call(
        functools.partial(_conf_conv_kernel, taps=taps, halo=halo, tl=tl),
        grid=(bsz, seq // tl),
        in_specs=[pl.BlockSpec((1, tl, d), lambda b, i: (b, i, 0)), prev_spec, next_spec,
                  pl.BlockSpec((taps, d), const), pl.BlockSpec((1, d), const),
                  pl.BlockSpec((1, d), const), pl.BlockSpec((1, d), const)],
        out_specs=pl.BlockSpec((1, tl, d), lambda b, i: (b, i, 0)),
        out_shape=jax.ShapeDtypeStruct((bsz, seq, d), BF16),
        scratch_shapes=[pltpu.VMEM((tl + 2 * halo, d), F32),
                        pltpu.VMEM((SUBLANES - 1, tl + 2 * halo, d), F32)],
        compiler_params=_params("parallel", "parallel"),
        name="conf_conv",
    )(glu, glu, glu, dw_w, dw_b, ln_g, ln_b)


def _ssd_kernel(xs_ref, b_ref, c_ref, dt_ref, dtt_ref, dtb_ref, dtbt_ref, alog_ref, alogt_ref, dsk_ref, s0_ref,
                y_ref, sf_ref, st_ref, *, rev, direction, add_skip):
    T = SSD_CHUNK
    H, P, N = SSD_HEADS, SSD_HEAD_DIM, SSD_STATE
    GW = SSD_INNER // SSD_GROUPS

    @pl.when(pl.program_id(1) == 0)
    def _():
        st_ref[...] = s0_ref[0]

    dt = _softplus(dt_ref[0][:, direction * H:(direction + 1) * H] + dtb_ref[...])
    a_col = dt * (-jnp.exp(alog_ref[...]))
    a_row = _softplus(dtt_ref[0] + dtbt_ref[...]) * (-jnp.exp(alogt_ref[...]))

    ti = lax.broadcasted_iota(jnp.int32, (T, T), 0)
    si = lax.broadcasted_iota(jnp.int32, (T, T), 1)
    valid = (si >= ti) if rev else (si <= ti)
    valid_f = valid.astype(F32)
    valid_t = ((ti >= si) if rev else (ti <= si)).astype(F32)
    acum = jnp.dot(valid_f, a_col, precision=HIGHEST, preferred_element_type=F32)
    acum_row = jnp.dot(a_row, valid_t, precision=HIGHEST, preferred_element_type=F32)
    atot = acum[0:1] if rev else acum[T - 1:T]

    e_p = (lax.broadcasted_iota(jnp.int32, (H, H * P), 1) // P
           == lax.broadcasted_iota(jnp.int32, (H, H * P), 0)).astype(BF16)
    expand = lambda t, passes: _split_dot(t, e_p, passes)
    dt_x = expand(dt, 2)
    eac_x = expand(jnp.exp(acum), 3)
    dec_x = expand(jnp.exp(atot - acum), 2)
    eat_x = expand(jnp.broadcast_to(jnp.exp(atot), (SUBLANES, H)), 3)[0:1]

    xs = xs_ref[0].astype(F32)
    xdt = xs * dt_x
    xdec = (xdt * dec_x).astype(BF16)
    bm = b_ref[0].astype(BF16)
    cm = c_ref[0].astype(BF16)
    lane = lax.broadcasted_iota(jnp.int32, (T, 2 * P), 1)

    y_parts = []
    for g in range(SSD_GROUPS):
        bg = bm[:, g * N:(g + 1) * N]
        cg = cm[:, g * N:(g + 1) * N]
        cb = lax.dot_general(cg, bg, (((1,), (1,)), ((), ())), preferred_element_type=F32)
        s_g = st_ref[:, g * GW:(g + 1) * GW]
        y_off = jnp.dot(cg, s_g.astype(BF16), preferred_element_type=F32) * eac_x[:, g * GW:(g + 1) * GW]
        upd = lax.dot_general(bg, xdec[:, g * GW:(g + 1) * GW], (((0,), (0,)), ((), ())),
                              preferred_element_type=F32)
        st_ref[:, g * GW:(g + 1) * GW] = s_g * eat_x[:, g * GW:(g + 1) * GW] + upd
        heads_per_group = H // SSD_GROUPS
        pair_parts = []
        for pp in range(heads_per_group // 2):
            l_pair = []
            for h in (g * heads_per_group + 2 * pp, g * heads_per_group + 2 * pp + 1):
                diff = acum[:, h:h + 1] - acum_row[h:h + 1, :]
                l_pair.append(jnp.where(valid, jnp.exp(jnp.minimum(diff, 0.0)), 0.0) * cb)
            lhs = jnp.concatenate(l_pair, axis=1).astype(BF16)
            col0 = (g * heads_per_group + 2 * pp) * P
            xp = xdt[:, col0:col0 + 2 * P]
            rhs = jnp.concatenate([jnp.where(lane < P, xp, 0.0), jnp.where(lane >= P, xp, 0.0)],
                                  axis=0).astype(BF16)
            pair_parts.append(jnp.dot(lhs, rhs, preferred_element_type=F32))
        y_parts.append(jnp.concatenate(pair_parts, axis=1) + y_off)
    y = jnp.concatenate(y_parts, axis=1)
    if add_skip:
        y = y + dsk_ref[...] * xs
    y_ref[0] = y.astype(y_ref.dtype)

    @pl.when(pl.program_id(1) == pl.num_programs(1) - 1)
    def _():
        sf_ref[0] = st_ref[...]


def ssd_scan(xbc, dt, dt_t, dt_bias, a_log, d_skip_x, s0, *, direction, add_skip):
    bsz, seq, _ = xbc.shape
    T, H = SSD_CHUNK, SSD_HEADS
    nc = seq // T
    rev = direction == 1
    chunk = (lambda c: nc - 1 - c) if rev else (lambda c: c)
    const = lambda b, c: (0, 0)
    y, sf = pl.pallas_call(
        functools.partial(_ssd_kernel, rev=rev, direction=direction, add_skip=add_skip),
        grid=(bsz, nc),
        in_specs=[pl.BlockSpec((1, T, SSD_INNER), lambda b, c: (b, chunk(c), 0)),
                  pl.BlockSpec((1, T, SSD_BC), lambda b, c: (b, chunk(c), SSD_INNER // SSD_BC)),
                  pl.BlockSpec((1, T, SSD_BC), lambda b, c: (b, chunk(c), SSD_INNER // SSD_BC + 1)),
                  pl.BlockSpec((1, T, DT_PAD), lambda b, c: (b, chunk(c), 0)),
                  pl.BlockSpec((1, H, T), lambda b, c: (b, direction, chunk(c))),
                  pl.BlockSpec((1, H), const), pl.BlockSpec((H, 1), const),
                  pl.BlockSpec((1, H), const), pl.BlockSpec((H, 1), const),
                  pl.BlockSpec((1, SSD_INNER), const),
                  pl.BlockSpec((1, SSD_STATE, SSD_INNER), lambda b, c: (b, 0, 0))],
        out_specs=[pl.BlockSpec((1, T, SSD_INNER), lambda b, c: (b, chunk(c), 0)),
                   pl.BlockSpec((1, SSD_STATE, SSD_INNER), lambda b, c: (b, 0, 0))],
        out_shape=[jax.ShapeDtypeStruct((bsz, seq, SSD_INNER), ACT),
                   jax.ShapeDtypeStruct((bsz, SSD_STATE, SSD_INNER), F32)],
        scratch_shapes=[pltpu.VMEM((SSD_STATE, SSD_INNER), F32)],
        compiler_params=_params("parallel", "arbitrary"),
        name=f"ssd_scan_d{direction}",
    )(xbc, xbc, xbc, dt, dt_t, dt_bias[None, :], dt_bias[:, None], a_log[None, :], a_log[:, None], d_skip_x, s0)
    return y, sf


def _gla_kernel(q_ref, f_ref, v_ref, lbraw_ref, s0_ref, o_ref, sf_ref, st_ref, c_scr, *, rev, first, layer):
    T, SB = HG_CHUNK, HG_SUB
    H, DK, DV = HG_HEADS, HG_DK, HG_DV
    W = H * DK
    NT = (((1,), (1,)), ((), ()))
    TN = (((0,), (0,)), ((), ()))

    @pl.when(pl.program_id(1) == 0)
    def _():
        st_ref[...] = s0_ref[0]

    a2 = f_ref[0].astype(F32) * LOG2E
    log_sig = jnp.minimum(a2, 0.0) - jnp.log(1.0 + jnp.exp2(-jnp.abs(a2))) * LOG2E
    log_sig_neg = log_sig - a2
    if first:
        lf, log_k = log_sig, log_sig_neg
    else:
        raw = lbraw_ref[...]
        pr = jnp.exp(raw - jnp.max(raw, axis=0, keepdims=True))
        pr = pr / jnp.sum(pr, axis=0, keepdims=True)
        lb = jnp.sum(pr[0:layer + 1], axis=0, keepdims=True) - pr[0:1]
        lf = jnp.log(lb + (1.0 - lb) * jnp.exp2(log_sig)) * LOG2E
        log_k = jnp.log(1.0 - lb) * LOG2E + log_sig_neg
    kk = jnp.exp2(log_k)
    q = _silu(q_ref[0].astype(F32))
    v_bf = v_ref[0].astype(BF16)

    ti = lax.broadcasted_iota(jnp.int32, (T, T), 0)
    si = lax.broadcasted_iota(jnp.int32, (T, T), 1)
    valid_f = ((si >= ti) if rev else (si <= ti)).astype(F32)
    acum = jnp.dot(valid_f, lf, precision=HIGHEST, preferred_element_type=F32)
    atot = acum[0:1] if rev else acum[T - 1:T]
    qe = (q * jnp.exp2(acum)).astype(BF16)
    kd = (kk * jnp.exp2(atot - acum)).astype(BF16)
    head = lambda t, h: t[:, h * DK:(h + 1) * DK]
    c_src = acum - log_k
    for h in range(H):
        c_scr[h] = head(c_src, h)

    level_p = []
    b = T // 2
    while b >= SB:
        q_parts, k_parts = [], []
        zeros = jnp.zeros((b, W), F32)
        for c0 in range(0, T, 2 * b):
            lo, hi = slice(c0, c0 + b), slice(c0 + b, c0 + 2 * b)
            src, tgt = (hi, lo) if rev else (lo, hi)
            ref_row = c0 + b if rev else c0 + b - 1
            r = acum[ref_row:ref_row + 1]
            q_t = q[tgt] * jnp.exp2(acum[tgt] - r)
            k_s = kk[src] * jnp.exp2(r - acum[src])
            q_parts += [q_t, zeros] if rev else [zeros, q_t]
            k_parts += [zeros, k_s] if rev else [k_s, zeros]
        q_l = jnp.concatenate(q_parts, axis=0).astype(BF16)
        k_l = jnp.concatenate(k_parts, axis=0).astype(BF16)
        level_p.append([lax.dot_general(head(q_l, h), head(k_l, h), NT, preferred_element_type=F32)
                        for h in range(H)])
        b //= 2

    inter = [lax.dot_general(head(qe, h), st_ref[h].astype(BF16), NT, preferred_element_type=F32) for h in range(H)]
    upd = [lax.dot_general(head(v_bf, h), head(kd, h), TN, preferred_element_type=F32) for h in range(H)]
    for h in range(H):
        st_ref[h] = st_ref[h] * jnp.exp2(head(atot, h)) + upd[h]

    row = lax.broadcasted_iota(jnp.int32, (SB, T), 0)
    lane = lax.broadcasted_iota(jnp.int32, (SB, T), 1)
    diag_p = [[] for _ in range(H)]
    for lo in range(0, T, SB):
        a_i, q_i = acum[lo:lo + SB], q[lo:lo + SB]
        p_h = [jnp.zeros((SB, T), F32) for _ in range(H)]
        for s in range(SB):
            here = (lane == lo + s) & ((row <= s) if rev else (row >= s))
            for h in range(H):
                c_s = jnp.broadcast_to(c_scr[h, lo + s:lo + s + 1, :], (SB, DK))
                w = head(q_i, h) * jnp.exp2(head(a_i, h) - c_s)
                p_h[h] = jnp.where(here, jnp.sum(w, axis=-1, keepdims=True), p_h[h])
        for h in range(H):
            diag_p[h].append(p_h[h])

    same = []
    b = T // 4
    while b >= SB:
        shift = (2 * b).bit_length() - 1
        same.append((ti >> shift) == (si >> shift))
        b //= 2
    out = []
    for h in range(H):
        p = level_p[0][h]
        for li, m in enumerate(same):
            p = jnp.where(m, level_p[li + 1][h], p)
        p = p + jnp.concatenate(diag_p[h], axis=0)
        out.append(jnp.dot(p.astype(BF16), head(v_bf, h), preferred_element_type=F32) + inter[h])
    o_ref[0] = jnp.concatenate(out, axis=1).astype(o_ref.dtype)

    @pl.when(pl.program_id(1) == pl.num_programs(1) - 1)
    def _():
        sf_ref[0] = st_ref[...]


def gla_scan(proj, lb_raw, s0, *, direction, first, layer):
    bsz, seq, _ = proj.shape
    T, H = HG_CHUNK, HG_HEADS
    nc = seq // T
    rev = direction == 1
    chunk = (lambda c: nc - 1 - c) if rev else (lambda c: c)
    o, sf = pl.pallas_call(
        functools.partial(_gla_kernel, rev=rev, first=first, layer=layer),
        grid=(bsz, nc),
        in_specs=[pl.BlockSpec((1, T, HG_INNER), lambda b, c: (b, chunk(c), COL_Q)),
                  pl.BlockSpec((1, T, HG_INNER), lambda b, c: (b, chunk(c), COL_F + direction)),
                  pl.BlockSpec((1, T, HG_INNER), lambda b, c: (b, chunk(c), COL_V)),
                  pl.BlockSpec(lb_raw.shape, lambda b, c: (0, 0)),
                  pl.BlockSpec((1, H, HG_DV, HG_DK), lambda b, c: (b, 0, 0, 0))],
        out_specs=[pl.BlockSpec((1, T, HG_INNER), lambda b, c: (b, chunk(c), 0)),
                   pl.BlockSpec((1, H, HG_DV, HG_DK), lambda b, c: (b, 0, 0, 0))],
        out_shape=[jax.ShapeDtypeStruct((bsz, seq, HG_INNER), ACT),
                   jax.ShapeDtypeStruct((bsz, H, HG_DV, HG_DK), F32)],
        scratch_shapes=[pltpu.VMEM((H, HG_DV, HG_DK), F32),
                        pltpu.VMEM((H, T, HG_DK), F32)],
        compiler_params=_params("parallel", "arbitrary"),
        name=f"gla_scan_d{direction}",
    )(proj, proj, proj, lb_raw, s0)
    return o, sf


def _mixout_kernel(yf_ref, yb_ref, z_ref, of_ref, ob_ref, g_ref, nws_ref, nwh_ref, w_ref,
                   x_ref, gate_ref, lg_ref, lb_ref, o_ref):
    def group_rms(t, groups):
        width = t.shape[-1] // groups
        parts = []
        for gi in range(groups):
            tg = t[:, gi * width:(gi + 1) * width]
            ms = jnp.mean(tg * tg, axis=-1, keepdims=True)
            parts.append(tg * lax.rsqrt(ms + EPS))
        return jnp.concatenate(parts, axis=1)

    f32 = lambda r: r[0].astype(F32)
    ys = group_rms((f32(yf_ref) + f32(yb_ref)) * _silu(f32(z_ref)), SSD_GROUPS) * nws_ref[...]
    os_ = group_rms(f32(of_ref) + f32(ob_ref), HG_HEADS) * nwh_ref[...] * _silu(f32(g_ref))
    a = jnp.concatenate([ys, os_], axis=1).astype(BF16)
    y = jnp.dot(a, w_ref[...], preferred_element_type=F32)
    o_ref[0] = _res_ln(x_ref[0], gate_ref[0], y, lg_ref[...], lb_ref[...])


def mix_out(y_f, y_b, o_f, o_b, proj, ssd_nw, hg_nw, w_out, x, modl, ctx, ln_g, ln_b):
    bsz, seq, d = x.shape
    tm = min(seq, 256)
    row = lambda b, i: (b, i, 0)
    const = lambda b, i: (0, 0)
    act = pl.BlockSpec((1, tm, d), row)
    return pl.pallas_call(
        _mixout_kernel,
        grid=(bsz, seq // tm),
        in_specs=[act, act, pl.BlockSpec((1, tm, d), lambda b, i: (b, i, COL_Z)), act, act,
                  pl.BlockSpec((1, tm, d), lambda b, i: (b, i, COL_G)),
                  pl.BlockSpec((1, d), const), pl.BlockSpec((1, d), const),
                  pl.BlockSpec(w_out.shape, const), act, _mod_spec(2, ctx),
                  pl.BlockSpec((1, d), const), pl.BlockSpec((1, d), const)],
        out_specs=act,
        out_shape=jax.ShapeDtypeStruct((bsz, seq, d), F32),
        compiler_params=_params("parallel", "parallel"),
        name="mix_out",
    )(y_f, y_b, proj, o_f, o_b, proj, ssd_nw, hg_nw, w_out, x, modl, ln_g, ln_b)


def _ffn_kernel(x_ref, xp_ref, xn_ref, sh_ref, sc_ref, gate_ref, wv0_ref, wg0_ref, wv_ref, wg_ref, cw_ref, cb_ref,
                wd_ref, lg_ref, lb_ref, o_ref, xm_ref, g_a, g_b, v_a, v_b, c0_ref, c2_ref, acc_ref,
                *, rows, cols, tl, halo, rb):
    i, j = pl.program_id(1), pl.program_id(2)
    last_i, last_j = pl.num_programs(1) - 1, pl.num_programs(2) - 1
    pad = SUBLANES
    ext = tl + 2 * halo
    tc = wd_ref.shape[0]

    n_parts = tl // rb

    def up_project(wv, wg, g_dst, v_dst, part=None):
        parts = range(n_parts) if part is None else (part,)
        for k in parts:
            v_lo, g_lo, g_n = k * rb, k * (ext // n_parts), ext // n_parts
            v_dst[v_lo:v_lo + rb, :] = jnp.dot(xm_ref[halo + v_lo:halo + v_lo + rb, :], wv[...],
                                               preferred_element_type=F32)
            g = jnp.dot(xm_ref[g_lo:g_lo + g_n, :], wg[...], preferred_element_type=F32)
            if halo:
                r = lax.broadcasted_iota(jnp.int32, (g_n, 1), 0) + g_lo
                inside = jnp.logical_and(jnp.logical_or(i > 0, r >= halo),
                                         jnp.logical_or(i < last_i, r < halo + tl))
                g = jnp.where(inside, g, 0.0)
            g_dst[pad + g_lo:pad + g_lo + g_n, :] = g

    @pl.when(j == 0)
    def _():
        scale, shift = 1.0 + sc_ref[0], sh_ref[0]
        xm_ref[halo:halo + tl, :] = (x_ref[0] * scale + shift).astype(BF16)
        if halo:
            xm_ref[0:halo, :] = (xp_ref[0] * scale + shift).astype(BF16)
            xm_ref[halo + tl:ext, :] = (xn_ref[0] * scale + shift).astype(BF16)
        for g_buf in (g_a, g_b):
            g_buf[0:pad, :] = jnp.zeros((pad, tc), F32)
            g_buf[pad + ext:pad + ext + pad, :] = jnp.zeros((pad, tc), F32)
        acc_ref[...] = jnp.zeros_like(acc_ref)
        up_project(wv0_ref, wg0_ref, g_a, v_a)

    def step(g_cur, v_cur, g_nxt, v_nxt):
        n = rb + 2 * pad
        for t0 in range(0, tl, rb):
            up_project(wv_ref, wg_ref, g_nxt, v_nxt, part=t0 // rb)
            e0 = halo + t0
            mid = g_cur[e0:e0 + n, :]
            col_sums = []
            for k in range(3):
                c = cw_ref[3 + k:4 + k, :] * mid
                if rows > 1:
                    c = c + cw_ref[k:k + 1, :] * g_cur[e0 - cols:e0 - cols + n, :]
                    c = c + cw_ref[6 + k:7 + k, :] * g_cur[e0 + cols:e0 + cols + n, :]
                col_sums.append(c)
            c0_ref[...] = col_sums[0]
            c2_ref[...] = col_sums[2]
            col = (lax.broadcasted_iota(jnp.int32, (rb, 1), 0) + t0) & (cols - 1)
            u = (col_sums[1][pad:pad + rb] + cb_ref[...]
                 + jnp.where(col > 0, c0_ref[pad - 1:pad - 1 + rb, :], 0.0)
                 + jnp.where(col < cols - 1, c2_ref[pad + 1:pad + 1 + rb, :], 0.0))
            act = (0.5 * u * (1.0 + lax.erf(u * (2.0 ** -0.5)))) * v_cur[t0:t0 + rb, :]
            acc_ref[t0:t0 + rb, :] += jnp.dot(act.astype(BF16), wd_ref[...], preferred_element_type=F32)

    @pl.when(j % 2 == 0)
    def _():
        step(g_a, v_a, g_b, v_b)

    @pl.when(j % 2 == 1)
    def _():
        step(g_b, v_b, g_a, v_a)

    @pl.when(j == last_j)
    def _():
        o_ref[0] = _res_ln(x_ref[0], gate_ref[0], acc_ref[...], lg_ref[...], lb_ref[...])


def _ffn_block(x, modl, ctx, w_up, conv_w, conv_b, w_down, ln_g, ln_b, rows, cols):
    bsz, seq, d = x.shape
    assert seq == rows * cols and cols & (cols - 1) == 0
    tc = 256
    nj = D_FF // tc
    tl = min(seq, 1024)
    assert tl % cols == 0
    halo = LANES if rows > 1 else 0
    assert halo == 0 or (halo > cols and tl % halo == 0)
    hb = halo if halo else SUBLANES
    per = tl // hb
    last_blk = seq // hb - 1
    rb = min(tl, 256)
    ext = tl + 2 * halo
    const = lambda b, i, j: (0, 0)
    nxt = lambda j: jnp.minimum(j + 1, nj - 1)
    return pl.pallas_call(
        functools.partial(_ffn_kernel, rows=rows, cols=cols, tl=tl, halo=halo, rb=rb),
        grid=(bsz, seq // tl, nj),
        in_specs=[pl.BlockSpec((1, tl, d), lambda b, i, j: (b, i, 0)),
                  pl.BlockSpec((1, hb, d), lambda b, i, j: (b, jnp.maximum(i * per - 1, 0), 0)),
                  pl.BlockSpec((1, hb, d), lambda b, i, j: (b, jnp.minimum((i + 1) * per, last_blk), 0)),
                  _mod_spec(3, ctx), _mod_spec(4, ctx), _mod_spec(5, ctx),
                  pl.BlockSpec((d, tc), lambda b, i, j: (0, 0)),
                  pl.BlockSpec((d, tc), lambda b, i, j: (0, nj)),
                  pl.BlockSpec((d, tc), lambda b, i, j: (0, nxt(j))),
                  pl.BlockSpec((d, tc), lambda b, i, j: (0, nj + nxt(j))),
                  pl.BlockSpec((9, tc), lambda b, i, j: (0, j)),
                  pl.BlockSpec((1, tc), lambda b, i, j: (0, j)),
                  pl.BlockSpec((tc, d), lambda b, i, j: (j, 0)),
                  pl.BlockSpec((1, d), const), pl.BlockSpec((1, d), const)],
        out_specs=pl.BlockSpec((1, tl, d), lambda b, i, j: (b, i, 0)),
        out_shape=jax.ShapeDtypeStruct((bsz, seq, d), F32),
        scratch_shapes=[pltpu.VMEM((ext, d), BF16),
                        pltpu.VMEM((ext + 2 * SUBLANES, tc), F32), pltpu.VMEM((ext + 2 * SUBLANES, tc), F32),
                        pltpu.VMEM((tl, tc), F32), pltpu.VMEM((tl, tc), F32),
                        pltpu.VMEM((rb + 2 * SUBLANES, tc), F32),
                        pltpu.VMEM((rb + 2 * SUBLANES, tc), F32),
                        pltpu.VMEM((tl, d), F32)],
        compiler_params=_params("parallel", "parallel", "arbitrary"),
        name="ffn",
    )(x, x, x, modl, modl, modl, w_up, w_up, w_up, w_up, conv_w, conv_b, w_down, ln_g, ln_b)


def _in_proj(x, modl, ctx, w_main, w_dt, conv_w, conv_b):
    proj, dt = mod_matmul(x, modl, ctx, 0, 1, w_main, n_out=IN_MAIN, w_extra=w_dt, name="in_proj")
    xbc = xbc_conv(proj, conv_w, conv_b)
    dt_t = jnp.swapaxes(dt[:, :, :2 * SSD_HEADS], 1, 2)
    return proj, xbc, dt, dt_t


def kernel(x, c, ctx, c_ctx, ada_w, ada_b, post_ln_g, post_ln_b, mix_w_in, ssd_conv_w, ssd_conv_b, ssd_dt_bias,
           ssd_a_log, ssd_d, ssd_norm_w, hg_lb_raw, hg_norm_w, mix_w_out, conf_w1, conf_b1, conf_dw_w, conf_dw_b,
           conf_ln_g, conf_ln_b, conf_w2, conf_b2, ffn_w_up, ffn_conv_w, ffn_conv_b, ffn_w_down):
    bsz, seq, d = x.shape
    rows = seq // GRID_W
    ctx_len = ctx.shape[1]
    assert d == D_MODEL and bsz <= MOD_ROWS // 2

    cond = jnp.zeros((MOD_ROWS, d), F32).at[:bsz].set(c).at[MOD_ROWS // 2].set(c_ctx)
    mod = ada_table(cond, ada_w, ada_b).reshape(DEPTH, MOD_ROWS * N_MOD, 1, d)

    xc = ctx
    for l in range(DEPTH):
        modl = mod[l]
        ctx_next = any(j % 2 == 0 for j in range(l + 1, DEPTH))
        ln_g0, ln_b0 = post_ln_g[l, 0][None], post_ln_b[l, 0][None]
        ln_g1, ln_b1 = post_ln_g[l, 1][None], post_ln_b[l, 1][None]
        if l % 2 == 0:
            e = l // 2
            w_in = mix_w_in[e]
            o_z, o_xbc, o_dt = 0, SSD_INNER, SSD_INNER + SSD_XBC
            o_q = o_dt + 2 * SSD_HEADS
            w_main = jnp.concatenate([w_in[:, o_z:o_xbc], w_in[:, o_q:], w_in[:, o_xbc:o_dt]], axis=1).astype(BF16)
            w_dt = jnp.pad(w_in[:, o_dt:o_q], ((0, 0), (0, DT_PAD - 2 * SSD_HEADS))).astype(BF16)
            conv_w, conv_b = ssd_conv_w[e], ssd_conv_b[e][None]
            d_skip_x = jnp.repeat(ssd_d[e], SSD_HEAD_DIM)[None]
            w_out = mix_w_out[e].astype(BF16)
            lat = _in_proj(x, modl, False, w_main, w_dt, conv_w, conv_b)
            cx = _in_proj(xc, modl, True, w_main, w_dt, conv_w, conv_b)
            ys, os_ = {}, {}
            for dr in range(2):
                scan_args = dict(direction=dr, add_skip=dr == 0)
                zeros_ssd = jnp.zeros((bsz, SSD_STATE, SSD_INNER), F32)
                ys["c", dr], s_ctx = ssd_scan(cx[1], cx[2], cx[3], ssd_dt_bias[e, dr], ssd_a_log[e, dr], d_skip_x,
                                              zeros_ssd, **scan_args)
                ys["l", dr], _ = ssd_scan(lat[1], lat[2], lat[3], ssd_dt_bias[e, dr], ssd_a_log[e, dr], d_skip_x,
                                          s_ctx, **scan_args)
                zeros_hg = jnp.zeros((bsz, HG_HEADS, HG_DV, HG_DK), F32)
                os_["c", dr], st_ctx = gla_scan(cx[0], hg_lb_raw, zeros_hg, direction=dr, first=e == 0, layer=e)
                os_["l", dr], _ = gla_scan(lat[0], hg_lb_raw, st_ctx, direction=dr, first=e == 0, layer=e)
            nws, nwh = ssd_norm_w[e][None], hg_norm_w[e][None]
            x1 = mix_out(ys["l", 0], ys["l", 1], os_["l", 0], os_["l", 1], lat[0], nws, nwh, w_out,
                         x, modl, False, ln_g0, ln_b0)
            if ctx_next:
                xc1 = mix_out(ys["c", 0], ys["c", 1], os_["c", 0], os_["c", 1], cx[0], nws, nwh, w_out,
                              xc, modl, True, ln_g0, ln_b0)
        else:
            o = l // 2
            w1 = conf_w1[o].astype(BF16)
            w2 = conf_w2[o].astype(BF16)
            conv_args = (conf_dw_w[o], conf_dw_b[o][None], conf_ln_g[o][None], conf_ln_b[o][None])

            def conformer(t, is_ctx):
                glu = mod_matmul(t, modl, is_ctx, 0, 1, w1, n_out=d, glu=True, bias=conf_b1[o][None], name="conf_glu")
                u = conf_conv(glu, *conv_args)
                return matmul_res_ln(u, w2, conf_b2[o][None], t, modl, is_ctx, 2, ln_g0, ln_b0, name="conf_out")

            x1 = conformer(x, False)
            if ctx_next:
                xc1 = conformer(xc, True)
        ffn_args = (ffn_w_up[l].astype(BF16), ffn_conv_w[l].reshape(9, D_FF), ffn_conv_b[l][None],
                    ffn_w_down[l].astype(BF16), ln_g1, ln_b1)
        x = _ffn_block(x1, modl, False, *ffn_args, rows, GRID_W)
        if ctx_next:
            xc = _ffn_block(xc1, modl, True, *ffn_args, 1, ctx_len)
    return x
```

```python
import functools

import jax
import jax.numpy as jnp
from jax import lax
from jax.experimental import pallas as pl
from jax.experimental.pallas import tpu as pltpu

F32 = jnp.float32
BF16 = jnp.bfloat16
HIGHEST = lax.Precision.HIGHEST

D_MODEL = 1024
DEPTH = 4
GRID_W = 64
SSD_HEADS = 16
SSD_HEAD_DIM = 64
SSD_INNER = SSD_HEADS * SSD_HEAD_DIM
SSD_GROUPS = 2
SSD_STATE = 128
SSD_BC = SSD_GROUPS * SSD_STATE
SSD_XBC = SSD_INNER + 2 * SSD_BC
SSD_CHUNK = 128
HG_HEADS = 8
HG_DK = 128
HG_DV = D_MODEL // HG_HEADS
HG_INNER = HG_HEADS * HG_DK
HG_CHUNK = 128
HG_SUB = 8
D_FF = 2816
CONF_KERNEL = 31
ALPHA = (2 * DEPTH) ** 0.25
EPS = 1e-5
LOG2E = 1.4426950408889634
N_MOD = 6
MOD_ROWS = 8
DT_PAD = 128
LANES = 128
SUBLANES = 8
VMEM_LIMIT = 48 * 1024 * 1024
ACT = BF16
PACKED_ROWS = 16
MAX_TN = 1536

COL_Z, COL_Q, COL_F, COL_V, COL_G = 0, 1, 2, 4, 5
COL_XBC = 6 * D_MODEL
IN_MAIN = COL_XBC + SSD_XBC


def _params(*sem):
    return pltpu.CompilerParams(dimension_semantics=sem, vmem_limit_bytes=VMEM_LIMIT)


def _silu(t):
    return t * jax.nn.sigmoid(t)


def _softplus(t):
    return jnp.maximum(t, 0.0) + jnp.log1p(jnp.exp(-jnp.abs(t)))


def _split_dot(t, m, passes):
    acc, rem = None, t
    for i in range(passes):
        piece = rem.astype(BF16)
        part = jnp.dot(piece, m, preferred_element_type=F32)
        acc = part if acc is None else acc + part
        if i + 1 < passes:
            rem = rem - piece.astype(F32)
    return acc


def _res_ln(x, gate, y, g, b):
    r = ALPHA * x + gate * y
    mu = jnp.mean(r, axis=-1, keepdims=True)
    rc = r - mu
    var = jnp.mean(rc * rc, axis=-1, keepdims=True)
    return rc * lax.rsqrt(var + EPS) * g + b


def _mod_spec(k, ctx):
    if ctx:
        return pl.BlockSpec((1, 1, D_MODEL), lambda b, *_: ((MOD_ROWS // 2) * N_MOD + k, 0, 0))
    return pl.BlockSpec((1, 1, D_MODEL), lambda b, *_: (b * N_MOD + k, 0, 0))


def _ada_kernel(s_ref, w_ref, b_ref, o_ref):
    s = _silu(s_ref[...])
    o_ref[0] = jnp.dot(s, w_ref[0], precision=HIGHEST, preferred_element_type=F32) + b_ref[0]


def ada_table(s, ada_w, ada_b):
    depth, d, n = ada_w.shape
    tn = n // 4
    return pl.pallas_call(
        _ada_kernel,
        grid=(depth, n // tn),
        in_specs=[pl.BlockSpec((MOD_ROWS, d), lambda l, j: (0, 0)),
                  pl.BlockSpec((1, d, tn), lambda l, j: (l, 0, j)),
                  pl.BlockSpec((1, 1, tn), lambda l, j: (l, 0, j))],
        out_specs=pl.BlockSpec((1, MOD_ROWS, tn), lambda l, j: (l, 0, j)),
        out_shape=jax.ShapeDtypeStruct((depth, MOD_ROWS, n), F32),
        compiler_params=_params("parallel", "parallel"),
        name="ada_table",
    )(s, ada_w, ada_b.reshape(depth, 1, n))


def _mm_kernel(*refs, glu, has_bias, has_extra):
    it = iter(refs)
    x_ref, sh_ref, sc_ref, w_ref = next(it), next(it), next(it), next(it)
    wg_ref = next(it) if glu else None
    b_ref = next(it) if has_bias else None
    bg_ref = next(it) if (has_bias and glu) else None
    wx_ref = next(it) if has_extra else None
    o_ref = next(it)
    ox_ref = next(it) if has_extra else None
    xm_ref = next(it)

    @pl.when(pl.program_id(2) == 0)
    def _():
        xm = x_ref[0] * (1.0 + sc_ref[0]) + sh_ref[0]
        xm_ref[...] = xm.astype(BF16)
        if has_extra:
            ox_ref[0] = jnp.dot(xm_ref[...], wx_ref[...], preferred_element_type=F32)

    xm = xm_ref[...]
    acc = jnp.dot(xm, w_ref[...], preferred_element_type=F32)
    if has_bias:
        acc = acc + b_ref[...]
    if glu:
        gate = jnp.dot(xm, wg_ref[...], preferred_element_type=F32)
        if has_bias:
            gate = gate + bg_ref[...]
        acc = acc * jax.nn.sigmoid(gate)
    o_ref[0] = acc.astype(o_ref.dtype)


def mod_matmul(x, modl, ctx, k_shift, k_scale, w, *, n_out, glu=False, bias=None, w_extra=None, name):
    bsz, seq, d = x.shape
    tm = min(seq, 1024)
    tn = max(t for t in range(LANES, MAX_TN + 1, LANES) if n_out % t == 0)
    nj = n_out // tn
    in_specs = [pl.BlockSpec((1, tm, d), lambda b, i, j: (b, i, 0)),
                _mod_spec(k_shift, ctx), _mod_spec(k_scale, ctx),
                pl.BlockSpec((d, tn), lambda b, i, j: (0, j))]
    args = [x, modl, modl, w]
    if glu:
        in_specs.append(pl.BlockSpec((d, tn), lambda b, i, j: (0, j + nj)))
        args.append(w)
    if bias is not None:
        in_specs.append(pl.BlockSpec((1, tn), lambda b, i, j: (0, j)))
        args.append(bias)
        if glu:
            in_specs.append(pl.BlockSpec((1, tn), lambda b, i, j: (0, j + nj)))
            args.append(bias)
    out_shape = [jax.ShapeDtypeStruct((bsz, seq, n_out), ACT)]
    out_specs = [pl.BlockSpec((1, tm, tn), lambda b, i, j: (b, i, j))]
    if w_extra is not None:
        in_specs.append(pl.BlockSpec((d, DT_PAD), lambda b, i, j: (0, 0)))
        args.append(w_extra)
        out_shape.append(jax.ShapeDtypeStruct((bsz, seq, DT_PAD), F32))
        out_specs.append(pl.BlockSpec((1, tm, DT_PAD), lambda b, i, j: (b, i, 0)))
    res = pl.pallas_call(
        functools.partial(_mm_kernel, glu=glu, has_bias=bias is not None, has_extra=w_extra is not None),
        grid=(bsz, seq // tm, nj),
        in_specs=in_specs, out_specs=out_specs, out_shape=out_shape,
        scratch_shapes=[pltpu.VMEM((tm, d), BF16)],
        compiler_params=_params("parallel", "parallel", "arbitrary"),
        name=name,
    )(*args)
    return res if w_extra is not None else res[0]


def _resln_kernel(*refs, has_bias):
    it = iter(refs)
    a_ref, w_ref = next(it), next(it)
    b_ref = next(it) if has_bias else None
    x_ref, gate_ref, g_ref, beta_ref, o_ref = next(it), next(it), next(it), next(it), next(it)
    y = jnp.dot(a_ref[0], w_ref[...], preferred_element_type=F32)
    if has_bias:
        y = y + b_ref[...]
    o_ref[0] = _res_ln(x_ref[0], gate_ref[0], y, g_ref[...], beta_ref[...])


def matmul_res_ln(a, w, bias, x, modl, ctx, k_gate, ln_g, ln_b, *, name):
    bsz, seq, k = a.shape
    d = x.shape[-1]
    tm = min(seq, 512)
    row = lambda b, i: (b, i, 0)
    const = lambda b, i: (0, 0)
    in_specs = [pl.BlockSpec((1, tm, k), row), pl.BlockSpec((k, d), const)]
    args = [a, w]
    if bias is not None:
        in_specs.append(pl.BlockSpec((1, d), const))
        args.append(bias)
    in_specs += [pl.BlockSpec((1, tm, d), row), _mod_spec(k_gate, ctx),
                 pl.BlockSpec((1, d), const), pl.BlockSpec((1, d), const)]
    args += [x, modl, ln_g, ln_b]
    return pl.pallas_call(
        functools.partial(_resln_kernel, has_bias=bias is not None),
        grid=(bsz, seq // tm),
        in_specs=in_specs,
        out_specs=pl.BlockSpec((1, tm, d), row),
        out_shape=jax.ShapeDtypeStruct((bsz, seq, d), F32),
        compiler_params=_params("parallel", "parallel"),
        name=name,
    )(*args)


def _fill_halo(buf_ref, cur_ref, prev_ref, next_ref, halo, tl):
    i = pl.program_id(1)
    last = pl.num_programs(1) - 1
    buf_ref[0:halo] = jnp.where(i > 0, prev_ref[0].astype(F32), 0.0)
    buf_ref[halo:halo + tl] = cur_ref[0].astype(F32)
    buf_ref[halo + tl:halo + tl + halo] = jnp.where(i < last, next_ref[0].astype(F32), 0.0)


def _halo_specs(seq, tl, halo, width, col_block):
    per = tl // halo
    last = seq // halo - 1
    def prev_map(b, i, *rest):
        return (b, jnp.maximum(i * per - 1, 0), col_block(*rest))
    def next_map(b, i, *rest):
        return (b, jnp.minimum((i + 1) * per, last), col_block(*rest))
    return pl.BlockSpec((1, halo, width), prev_map), pl.BlockSpec((1, halo, width), next_map)


def _xbc_conv_kernel(cur_ref, prev_ref, next_ref, w_ref, b_ref, o_ref, buf_ref, *, taps, halo, tl):
    _fill_halo(buf_ref, cur_ref, prev_ref, next_ref, halo, tl)
    acc = jnp.broadcast_to(b_ref[...], (tl, b_ref.shape[-1]))
    for k in range(taps):
        off = halo - taps // 2 + k
        acc = acc + w_ref[k:k + 1, :] * buf_ref[off:off + tl, :]
    o_ref[0] = _silu(acc).astype(o_ref.dtype)


def xbc_conv(proj, conv_w, conv_b):
    bsz, seq, _ = proj.shape
    taps, width = conv_w.shape
    tc = 512
    tl = min(seq, 512)
    halo = PACKED_ROWS
    first = COL_XBC // tc
    prev_spec, next_spec = _halo_specs(seq, tl, halo, tc, lambda j: first + j)
    return pl.pallas_call(
        functools.partial(_xbc_conv_kernel, taps=taps, halo=halo, tl=tl),
        grid=(bsz, seq // tl, width // tc),
        in_specs=[pl.BlockSpec((1, tl, tc), lambda b, i, j: (b, i, first + j)), prev_spec, next_spec,
                  pl.BlockSpec((taps, tc), lambda b, i, j: (0, j)),
                  pl.BlockSpec((1, tc), lambda b, i, j: (0, j))],
        out_specs=pl.BlockSpec((1, tl, tc), lambda b, i, j: (b, i, j)),
        out_shape=jax.ShapeDtypeStruct((bsz, seq, width), ACT),
        scratch_shapes=[pltpu.VMEM((tl + 2 * halo, tc), F32)],
        compiler_params=_params("parallel", "parallel", "parallel"),
        name="xbc_conv",
    )(proj, proj, proj, conv_w, conv_b)


def _conf_conv_kernel(cur_ref, prev_ref, next_ref, w_ref, b_ref, g_ref, beta_ref, o_ref, buf_ref, ph_ref,
                      *, taps, halo, tl):
    _fill_halo(buf_ref, cur_ref, prev_ref, next_ref, halo, tl)
    rows = tl + 2 * halo
    full = buf_ref[...]
    for p in range(1, SUBLANES):
        ph_ref[p - 1] = pltpu.roll(full, rows - p, axis=0)
    acc = jnp.broadcast_to(b_ref[...], (tl, b_ref.shape[-1]))
    for k in range(taps):
        off = halo - taps // 2 + k
        p = off % SUBLANES
        base = off - p
        win = buf_ref[base:base + tl, :] if p == 0 else ph_ref[p - 1, base:base + tl, :]
        acc = acc + w_ref[k:k + 1, :] * win
    mu = jnp.mean(acc, axis=-1, keepdims=True)
    ac = acc - mu
    var = jnp.mean(ac * ac, axis=-1, keepdims=True)
    u = ac * lax.rsqrt(var + EPS) * g_ref[...] + beta_ref[...]
    o_ref[0] = _silu(u).astype(o_ref.dtype)


def conf_conv(glu, dw_w, dw_b, ln_g, ln_b):
    bsz, seq, d = glu.shape
    taps = dw_w.shape[0]
    tl = min(seq, 256)
    halo = PACKED_ROWS
    prev_spec, next_spec = _halo_specs(seq, tl, halo, d, lambda: 0)
    const = lambda b, i: (0, 0)
    return pl.pallas_call(
        functools.partial(_conf_conv_kernel, taps=taps, halo=halo, tl=tl),
        grid=(bsz, seq // tl),
        in_specs=[pl.BlockSpec((1, tl, d), lambda b, i: (b, i, 0)), prev_spec, next_spec,
                  pl.BlockSpec((taps, d), const), pl.BlockSpec((1, d), const),
                  pl.BlockSpec((1, d), const), pl.BlockSpec((1, d), const)],
        out_specs=pl.BlockSpec((1, tl, d), lambda b, i: (b, i, 0)),
        out_shape=jax.ShapeDtypeStruct((bsz, seq, d), BF16),
        scratch_shapes=[pltpu.VMEM((tl + 2 * halo, d), F32),
                        pltpu.VMEM((SUBLANES - 1, tl + 2 * halo, d), F32)],
        compiler_params=_params("parallel", "parallel"),
        name="conf_conv",
    )(glu, glu, glu, dw_w, dw_b, ln_g, ln_b)


def _ssd_kernel(xs_ref, b_ref, c_ref, dt_ref, dtt_ref, dtb_ref, dtbt_ref, alog_ref, alogt_ref, dsk_ref, s0_ref,
                y_ref, sf_ref, st_ref, *, rev, direction, add_skip):
    T = SSD_CHUNK
    H, P, N = SSD_HEADS, SSD_HEAD_DIM, SSD_STATE
    GW = SSD_INNER // SSD_GROUPS

    @pl.when(pl.program_id(1) == 0)
    def _():
        st_ref[...] = s0_ref[0]

    dt = _softplus(dt_ref[0][:, direction * H:(direction + 1) * H] + dtb_ref[...])
    a_col = dt * (-jnp.exp(alog_ref[...]))
    dt_row = _softplus(dtt_ref[0] + dtbt_ref[...])
    a_row = dt_row * (-jnp.exp(alogt_ref[...]))

    ti = lax.broadcasted_iota(jnp.int32, (T, T), 0)
    si = lax.broadcasted_iota(jnp.int32, (T, T), 1)
    valid = (si >= ti) if rev else (si <= ti)
    valid_f = valid.astype(F32)
    valid_t = ((ti >= si) if rev else (ti <= si)).astype(F32)
    acum = jnp.dot(valid_f, a_col, precision=HIGHEST, preferred_element_type=F32)
    acum_row = jnp.dot(a_row, valid_t, precision=HIGHEST, preferred_element_type=F32)
    atot = acum[0:1] if rev else acum[T - 1:T]

    e_p = (lax.broadcasted_iota(jnp.int32, (H, H * P), 1) // P
           == lax.broadcasted_iota(jnp.int32, (H, H * P), 0)).astype(BF16)
    expand = lambda t, passes: _split_dot(t, e_p, passes)
    eac_x = expand(jnp.exp(acum), 3)
    dtdec_x = expand(dt * jnp.exp(atot - acum), 2)
    eat_x = expand(jnp.broadcast_to(jnp.exp(atot), (SUBLANES, H)), 3)[0:1]

    xs = xs_ref[0].astype(F32)
    xdec = (xs * dtdec_x).astype(BF16)
    bm = b_ref[0].astype(BF16)
    cm = c_ref[0].astype(BF16)
    lane = lax.broadcasted_iota(jnp.int32, (T, 2 * P), 1)

    cbs, y_offs = [], []
    for g in range(SSD_GROUPS):
        bg = bm[:, g * N:(g + 1) * N]
        cg = cm[:, g * N:(g + 1) * N]
        cbs.append(lax.dot_general(cg, bg, (((1,), (1,)), ((), ())), preferred_element_type=F32))
        s_g = st_ref[:, g * GW:(g + 1) * GW]
        y_offs.append(jnp.dot(cg, s_g.astype(BF16), preferred_element_type=F32))
        upd = lax.dot_general(bg, xdec[:, g * GW:(g + 1) * GW], (((0,), (0,)), ((), ())),
                              preferred_element_type=F32)
        st_ref[:, g * GW:(g + 1) * GW] = s_g * eat_x[:, g * GW:(g + 1) * GW] + upd
    pairs = []
    for pp in range(H // 2):
        l_pair = []
        for h in (2 * pp, 2 * pp + 1):
            diff = acum[:, h:h + 1] - acum_row[h:h + 1, :]
            decay = jnp.where(valid, jnp.exp(jnp.minimum(diff, 0.0)), 0.0)
            l_pair.append(decay * (cbs[h // (H // SSD_GROUPS)] * dt_row[h:h + 1, :]))
        lhs = jnp.concatenate(l_pair, axis=1).astype(BF16)
        xp = xs[:, 2 * pp * P:(2 * pp + 2) * P]
        rhs = jnp.concatenate([jnp.where(lane < P, xp, 0.0), jnp.where(lane >= P, xp, 0.0)],
                              axis=0).astype(BF16)
        pairs.append((lhs, rhs))
    y_diag = jnp.concatenate([jnp.dot(lhs, rhs, preferred_element_type=F32) for lhs, rhs in pairs], axis=1)
    y = y_diag + jnp.concatenate(y_offs, axis=1) * eac_x
    if add_skip:
        y = y + dsk_ref[...] * xs
    y_ref[0] = y.astype(y_ref.dtype)

    @pl.when(pl.program_id(1) == pl.num_programs(1) - 1)
    def _():
        sf_ref[0] = st_ref[...]


def ssd_scan(xbc, dt, dt_t, dt_bias, a_log, d_skip_x, s0, *, direction, add_skip):
    bsz, seq, _ = xbc.shape
    T, H = SSD_CHUNK, SSD_HEADS
    nc = seq // T
    rev = direction == 1
    chunk = (lambda c: nc - 1 - c) if rev else (lambda c: c)
    const = lambda b, c: (0, 0)
    y, sf = pl.pallas_call(
        functools.partial(_ssd_kernel, rev=rev, direction=direction, add_skip=add_skip),
        grid=(bsz, nc),
        in_specs=[pl.BlockSpec((1, T, SSD_INNER), lambda b, c: (b, chunk(c), 0)),
                  pl.BlockSpec((1, T, SSD_BC), lambda b, c: (b, chunk(c), SSD_INNER // SSD_BC)),
                  pl.BlockSpec((1, T, SSD_BC), lambda b, c: (b, chunk(c), SSD_INNER // SSD_BC + 1)),
                  pl.BlockSpec((1, T, DT_PAD), lambda b, c: (b, chunk(c), 0)),
                  pl.BlockSpec((1, H, T), lambda b, c: (b, direction, chunk(c))),
                  pl.BlockSpec((1, H), const), pl.BlockSpec((H, 1), const),
                  pl.BlockSpec((1, H), const), pl.BlockSpec((H, 1), const),
                  pl.BlockSpec((1, SSD_INNER), const),
                  pl.BlockSpec((1, SSD_STATE, SSD_INNER), lambda b, c: (b, 0, 0))],
        out_specs=[pl.BlockSpec((1, T, SSD_INNER), lambda b, c: (b, chunk(c), 0)),
                   pl.BlockSpec((1, SSD_STATE, SSD_INNER), lambda b, c: (b, 0, 0))],
        out_shape=[jax.ShapeDtypeStruct((bsz, seq, SSD_INNER), ACT),
                   jax.ShapeDtypeStruct((bsz, SSD_STATE, SSD_INNER), F32)],
        scratch_shapes=[pltpu.VMEM((SSD_STATE, SSD_INNER), F32)],
        compiler_params=_params("parallel", "arbitrary"),
        name=f"ssd_scan_d{direction}",
    )(xbc, xbc, xbc, dt, dt_t, dt_bias[None, :], dt_bias[:, None], a_log[None, :], a_log[:, None], d_skip_x, s0)
    return y, sf


def _gla_kernel(q_ref, f_ref, v_ref, lbraw_ref, s0_ref, o_ref, sf_ref, st_ref, c_scr, *, rev, first, layer):
    T, SB = HG_CHUNK, HG_SUB
    H, DK, DV = HG_HEADS, HG_DK, HG_DV
    W = H * DK
    NT = (((1,), (1,)), ((), ()))
    TN = (((0,), (0,)), ((), ()))

    @pl.when(pl.program_id(1) == 0)
    def _():
        st_ref[...] = s0_ref[0]

    a2 = f_ref[0].astype(F32) * LOG2E
    log_sig = jnp.minimum(a2, 0.0) - jnp.log(1.0 + jnp.exp2(-jnp.abs(a2))) * LOG2E
    log_sig_neg = log_sig - a2
    if first:
        lf, log_k = log_sig, log_sig_neg
    else:
        raw = lbraw_ref[...]
        pr = jnp.exp(raw - jnp.max(raw, axis=0, keepdims=True))
        pr = pr / jnp.sum(pr, axis=0, keepdims=True)
        lb = jnp.sum(pr[0:layer + 1], axis=0, keepdims=True) - pr[0:1]
        lf = jnp.log(lb + (1.0 - lb) * jnp.exp2(log_sig)) * LOG2E
        log_k = jnp.log(1.0 - lb) * LOG2E + log_sig_neg
    kk = jnp.exp2(log_k)
    q = _silu(q_ref[0].astype(F32))
    v_bf = v_ref[0].astype(BF16)

    ti = lax.broadcasted_iota(jnp.int32, (T, T), 0)
    si = lax.broadcasted_iota(jnp.int32, (T, T), 1)
    valid_f = ((si >= ti) if rev else (si <= ti)).astype(F32)
    acum = jnp.dot(valid_f, lf, precision=HIGHEST, preferred_element_type=F32)
    atot = acum[0:1] if rev else acum[T - 1:T]
    qe = (q * jnp.exp2(acum)).astype(BF16)
    kd = (kk * jnp.exp2(atot - acum)).astype(BF16)
    head = lambda t, h: t[:, h * DK:(h + 1) * DK]
    c_src = acum - log_k
    for h in range(H):
        c_scr[h] = head(c_src, h)

    level_p = []
    b = T // 2
    while b >= SB:
        q_parts, k_parts = [], []
        zeros = jnp.zeros((b, W), F32)
        for c0 in range(0, T, 2 * b):
            lo, hi = slice(c0, c0 + b), slice(c0 + b, c0 + 2 * b)
            src, tgt = (hi, lo) if rev else (lo, hi)
            ref_row = c0 + b if rev else c0 + b - 1
            r = acum[ref_row:ref_row + 1]
            q_t = q[tgt] * jnp.exp2(acum[tgt] - r)
            k_s = kk[src] * jnp.exp2(r - acum[src])
            q_parts += [q_t, zeros] if rev else [zeros, q_t]
            k_parts += [zeros, k_s] if rev else [k_s, zeros]
        q_l = jnp.concatenate(q_parts, axis=0).astype(BF16)
        k_l = jnp.concatenate(k_parts, axis=0).astype(BF16)
        level_p.append([lax.dot_general(head(q_l, h), head(k_l, h), NT, preferred_element_type=F32)
                        for h in range(H)])
        b //= 2

    inter = [lax.dot_general(head(qe, h), st_ref[h].astype(BF16), NT, preferred_element_type=F32) for h in range(H)]
    upd = [lax.dot_general(head(v_bf, h), head(kd, h), TN, preferred_element_type=F32) for h in range(H)]
    for h in range(H):
        st_ref[h] = st_ref[h] * jnp.exp2(head(atot, h)) + upd[h]

    row = lax.broadcasted_iota(jnp.int32, (SB, T), 0)
    lane = lax.broadcasted_iota(jnp.int32, (SB, T), 1)
    diag_p = [[] for _ in range(H)]
    for lo in range(0, T, SB):
        a_i, q_i = acum[lo:lo + SB], q[lo:lo + SB]
        p_h = [jnp.zeros((SB, T), F32) for _ in range(H)]
        for s in range(SB):
            here = (lane == lo + s) & ((row <= s) if rev else (row >= s))
            for h in range(H):
                c_s = jnp.broadcast_to(c_scr[h, lo + s:lo + s + 1, :], (SB, DK))
                w = head(q_i, h) * jnp.exp2(head(a_i, h) - c_s)
                p_h[h] = jnp.where(here, jnp.sum(w, axis=-1, keepdims=True), p_h[h])
        for h in range(H):
            diag_p[h].append(p_h[h])

    same = []
    b = T // 4
    while b >= SB:
        shift = (2 * b).bit_length() - 1
        same.append((ti >> shift) == (si >> shift))
        b //= 2
    out = []
    for h in range(H):
        p = level_p[0][h]
        for li, m in enumerate(same):
            p = jnp.where(m, level_p[li + 1][h], p)
        p = p + jnp.concatenate(diag_p[h], axis=0)
        out.append(jnp.dot(p.astype(BF16), head(v_bf, h), preferred_element_type=F32) + inter[h])
    o_ref[0] = jnp.concatenate(out, axis=1).astype(o_ref.dtype)

    @pl.when(pl.program_id(1) == pl.num_programs(1) - 1)
    def _():
        sf_ref[0] = st_ref[...]


def gla_scan(proj, lb_raw, s0, *, direction, first, layer):
    bsz, seq, _ = proj.shape
    T, H = HG_CHUNK, HG_HEADS
    nc = seq // T
    rev = direction == 1
    chunk = (lambda c: nc - 1 - c) if rev else (lambda c: c)
    o, sf = pl.pallas_call(
        functools.partial(_gla_kernel, rev=rev, first=first, layer=layer),
        grid=(bsz, nc),
        in_specs=[pl.BlockSpec((1, T, HG_INNER), lambda b, c: (b, chunk(c), COL_Q)),
                  pl.BlockSpec((1, T, HG_INNER), lambda b, c: (b, chunk(c), COL_F + direction)),
                  pl.BlockSpec((1, T, HG_INNER), lambda b, c: (b, chunk(c), COL_V)),
                  pl.BlockSpec(lb_raw.shape, lambda b, c: (0, 0)),
                  pl.BlockSpec((1, H, HG_DV, HG_DK), lambda b, c: (b, 0, 0, 0))],
        out_specs=[pl.BlockSpec((1, T, HG_INNER), lambda b, c: (b, chunk(c), 0)),
                   pl.BlockSpec((1, H, HG_DV, HG_DK), lambda b, c: (b, 0, 0, 0))],
        out_shape=[jax.ShapeDtypeStruct((bsz, seq, HG_INNER), ACT),
                   jax.ShapeDtypeStruct((bsz, H, HG_DV, HG_DK), F32)],
        scratch_shapes=[pltpu.VMEM((H, HG_DV, HG_DK), F32),
                        pltpu.VMEM((H, T, HG_DK), F32)],
        compiler_params=_params("parallel", "arbitrary"),
        name=f"gla_scan_d{direction}",
    )(proj, proj, proj, lb_raw, s0)
    return o, sf


def _mixout_kernel(yf_ref, yb_ref, z_ref, of_ref, ob_ref, g_ref, nws_ref, nwh_ref, w_ref,
                   x_ref, gate_ref, lg_ref, lb_ref, o_ref):
    def group_rms(t, groups):
        width = t.shape[-1] // groups
        parts = []
        for gi in range(groups):
            tg = t[:, gi * width:(gi + 1) * width]
            ms = jnp.mean(tg * tg, axis=-1, keepdims=True)
            parts.append(tg * lax.rsqrt(ms + EPS))
        return jnp.concatenate(parts, axis=1)

    f32 = lambda r: r[0].astype(F32)
    ys = group_rms((f32(yf_ref) + f32(yb_ref)) * _silu(f32(z_ref)), SSD_GROUPS) * nws_ref[...]
    os_ = group_rms(f32(of_ref) + f32(ob_ref), HG_HEADS) * nwh_ref[...] * _silu(f32(g_ref))
    a = jnp.concatenate([ys, os_], axis=1).astype(BF16)
    y = jnp.dot(a, w_ref[...], preferred_element_type=F32)
    o_ref[0] = _res_ln(x_ref[0], gate_ref[0], y, lg_ref[...], lb_ref[...])


def mix_out(y_f, y_b, o_f, o_b, proj, ssd_nw, hg_nw, w_out, x, modl, ctx, ln_g, ln_b):
    bsz, seq, d = x.shape
    tm = min(seq, 256)
    row = lambda b, i: (b, i, 0)
    const = lambda b, i: (0, 0)
    act = pl.BlockSpec((1, tm, d), row)
    return pl.pallas_call(
        _mixout_kernel,
        grid=(bsz, seq // tm),
        in_specs=[act, act, pl.BlockSpec((1, tm, d), lambda b, i: (b, i, COL_Z)), act, act,
                  pl.BlockSpec((1, tm, d), lambda b, i: (b, i, COL_G)),
                  pl.BlockSpec((1, d), const), pl.BlockSpec((1, d), const),
                  pl.BlockSpec(w_out.shape, const), act, _mod_spec(2, ctx),
                  pl.BlockSpec((1, d), const), pl.BlockSpec((1, d), const)],
        out_specs=act,
        out_shape=jax.ShapeDtypeStruct((bsz, seq, d), F32),
        compiler_params=_params("parallel", "parallel"),
        name="mix_out",
    )(y_f, y_b, proj, o_f, o_b, proj, ssd_nw, hg_nw, w_out, x, modl, ln_g, ln_b)


def _ffn_kernel(x_ref, xp_ref, xn_ref, sh_ref, sc_ref, gate_ref, wv0_ref, wg0_ref, wv_ref, wg_ref, cw_ref, cb_ref,
                wd_ref, lg_ref, lb_ref, o_ref, xm_ref, g_a, g_b, v_a, v_b, c0_ref, c2_ref, acc_ref,
                *, rows, cols, tl, halo, rb, n_tiles):
    i, j = pl.program_id(1), pl.program_id(2)
    last_i, last_j = pl.num_programs(1) - 1, pl.num_programs(2) - 1
    pad = SUBLANES
    ext = tl + 2 * halo
    tc = wd_ref.shape[0]

    n_parts = tl // rb

    def up_project(wv, wg, g_dst, v_dst, part=None):
        parts = range(n_parts) if part is None else (part,)
        for k in parts:
            v_lo, g_lo, g_n = k * rb, k * (ext // n_parts), ext // n_parts
            v_dst[v_lo:v_lo + rb, :] = jnp.dot(xm_ref[halo + v_lo:halo + v_lo + rb, :], wv[...],
                                               preferred_element_type=F32)
            g = jnp.dot(xm_ref[g_lo:g_lo + g_n, :], wg[...], preferred_element_type=F32)
            if halo:
                r = lax.broadcasted_iota(jnp.int32, (g_n, 1), 0) + g_lo
                inside = jnp.logical_and(jnp.logical_or(i > 0, r >= halo),
                                         jnp.logical_or(i < last_i, r < halo + tl))
                g = jnp.where(inside, g, 0.0)
            g_dst[pad + g_lo:pad + g_lo + g_n, :] = g

    @pl.when(j == 0)
    def _():
        scale, shift = 1.0 + sc_ref[0], sh_ref[0]
        xm_ref[halo:halo + tl, :] = (x_ref[0] * scale + shift).astype(BF16)
        if halo:
            xm_ref[0:halo, :] = (xp_ref[0] * scale + shift).astype(BF16)
            xm_ref[halo + tl:ext, :] = (xn_ref[0] * scale + shift).astype(BF16)
        for g_buf in (g_a, g_b):
            g_buf[0:pad, :] = jnp.zeros((pad, tc), F32)
            g_buf[pad + ext:pad + ext + pad, :] = jnp.zeros((pad, tc), F32)
        acc_ref[...] = jnp.zeros_like(acc_ref)
        up_project(wv0_ref, wg0_ref, g_a, v_a)

    def step(g_cur, v_cur, g_nxt, v_nxt, with_up=True):
        n = rb + 2 * pad
        for t0 in range(0, tl, rb):
            if with_up:
                up_project(wv_ref, wg_ref, g_nxt, v_nxt, part=t0 // rb)
            e0 = halo + t0
            mid = g_cur[e0:e0 + n, :]
            col_sums = []
            for k in range(3):
                c = cw_ref[3 + k:4 + k, :] * mid
                if rows > 1:
                    c = c + cw_ref[k:k + 1, :] * g_cur[e0 - cols:e0 - cols + n, :]
                    c = c + cw_ref[6 + k:7 + k, :] * g_cur[e0 + cols:e0 + cols + n, :]
                col_sums.append(c)
            c0_ref[...] = col_sums[0]
            c2_ref[...] = col_sums[2]
            col = (lax.broadcasted_iota(jnp.int32, (rb, 1), 0) + t0) & (cols - 1)
            u = (col_sums[1][pad:pad + rb] + cb_ref[...]
                 + jnp.where(col > 0, c0_ref[pad - 1:pad - 1 + rb, :], 0.0)
                 + jnp.where(col < cols - 1, c2_ref[pad + 1:pad + 1 + rb, :], 0.0))
            act = (0.5 * u * (1.0 + lax.erf(u * (2.0 ** -0.5)))) * v_cur[t0:t0 + rb, :]
            acc_ref[t0:t0 + rb, :] += jnp.dot(act.astype(BF16), wd_ref[...], preferred_element_type=F32)

    @pl.when(jnp.logical_and(j % 2 == 0, j < last_j))
    def _():
        step(g_a, v_a, g_b, v_b)

    @pl.when(jnp.logical_and(j % 2 == 1, j < last_j))
    def _():
        step(g_b, v_b, g_a, v_a)

    @pl.when(j == last_j)
    def _():
        if (n_tiles - 1) % 2 == 0:
            step(g_a, v_a, g_b, v_b, with_up=False)
        else:
            step(g_b, v_b, g_a, v_a, with_up=False)
        o_ref[0] = _res_ln(x_ref[0], gate_ref[0], acc_ref[...], lg_ref[...], lb_ref[...])


def _ffn_block(x, modl, ctx, w_up, conv_w, conv_b, w_down, ln_g, ln_b, rows, cols):
    bsz, seq, d = x.shape
    assert seq == rows * cols and cols & (cols - 1) == 0
    tc = 256
    nj = D_FF // tc
    tl = min(seq, 1024)
    assert tl % cols == 0
    halo = LANES if rows > 1 else 0
    assert halo == 0 or (halo > cols and tl % halo == 0)
    hb = halo if halo else SUBLANES
    per = tl // hb
    last_blk = seq // hb - 1
    rb = min(tl, 256)
    ext = tl + 2 * halo
    const = lambda b, i, j: (0, 0)
    nxt = lambda j: jnp.minimum(j + 1, nj - 1)
    return pl.pallas_call(
        functools.partial(_ffn_kernel, rows=rows, cols=cols, tl=tl, halo=halo, rb=rb, n_tiles=nj),
        grid=(bsz, seq // tl, nj),
        in_specs=[pl.BlockSpec((1, tl, d), lambda b, i, j: (b, i, 0)),
                  pl.BlockSpec((1, hb, d), lambda b, i, j: (b, jnp.maximum(i * per - 1, 0), 0)),
                  pl.BlockSpec((1, hb, d), lambda b, i, j: (b, jnp.minimum((i + 1) * per, last_blk), 0)),
                  _mod_spec(3, ctx), _mod_spec(4, ctx), _mod_spec(5, ctx),
                  pl.BlockSpec((d, tc), lambda b, i, j: (0, 0)),
                  pl.BlockSpec((d, tc), lambda b, i, j: (0, nj)),
                  pl.BlockSpec((d, tc), lambda b, i, j: (0, nxt(j))),
                  pl.BlockSpec((d, tc), lambda b, i, j: (0, nj + nxt(j))),
                  pl.BlockSpec((9, tc), lambda b, i, j: (0, j)),
                  pl.BlockSpec((1, tc), lambda b, i, j: (0, j)),
                  pl.BlockSpec((tc, d), lambda b, i, j: (j, 0)),
                  pl.BlockSpec((1, d), const), pl.BlockSpec((1, d), const)],
        out_specs=pl.BlockSpec((1, tl, d), lambda b, i, j: (b, i, 0)),
        out_shape=jax.ShapeDtypeStruct((bsz, seq, d), F32),
        scratch_shapes=[pltpu.VMEM((ext, d), BF16),
                        pltpu.VMEM((ext + 2 * SUBLANES, tc), F32), pltpu.VMEM((ext + 2 * SUBLANES, tc), F32),
                        pltpu.VMEM((tl, tc), F32), pltpu.VMEM((tl, tc), F32),
                        pltpu.VMEM((rb + 2 * SUBLANES, tc), F32),
                        pltpu.VMEM((rb + 2 * SUBLANES, tc), F32),
                        pltpu.VMEM((tl, d), F32)],
        compiler_params=_params("parallel", "parallel", "arbitrary"),
        name="ffn",
    )(x, x, x, modl, modl, modl, w_up, w_up, w_up, w_up, conv_w, conv_b, w_down, ln_g, ln_b)


def _in_proj(x, modl, ctx, w_main, w_dt, conv_w, conv_b):
    proj, dt = mod_matmul(x, modl, ctx, 0, 1, w_main, n_out=IN_MAIN, w_extra=w_dt, name="in_proj")
    xbc = xbc_conv(proj, conv_w, conv_b)
    dt_t = jnp.swapaxes(dt[:, :, :2 * SSD_HEADS], 1, 2)
    return proj, xbc, dt, dt_t


def kernel(x, c, ctx, c_ctx, ada_w, ada_b, post_ln_g, post_ln_b, mix_w_in, ssd_conv_w, ssd_conv_b, ssd_dt_bias,
           ssd_a_log, ssd_d, ssd_norm_w, hg_lb_raw, hg_norm_w, mix_w_out, conf_w1, conf_b1, conf_dw_w, conf_dw_b,
           conf_ln_g, conf_ln_b, conf_w2, conf_b2, ffn_w_up, ffn_conv_w, ffn_conv_b, ffn_w_down):
    bsz, seq, d = x.shape
    rows = seq // GRID_W
    ctx_len = ctx.shape[1]
    assert d == D_MODEL and bsz <= MOD_ROWS // 2

    cond = jnp.zeros((MOD_ROWS, d), F32).at[:bsz].set(c).at[MOD_ROWS // 2].set(c_ctx)
    mod = ada_table(cond, ada_w, ada_b).reshape(DEPTH, MOD_ROWS * N_MOD, 1, d)

    xc = ctx
    for l in range(DEPTH):
        modl = mod[l]
        ctx_next = any(j % 2 == 0 for j in range(l + 1, DEPTH))
        ln_g0, ln_b0 = post_ln_g[l, 0][None], post_ln_b[l, 0][None]
        ln_g1, ln_b1 = post_ln_g[l, 1][None], post_ln_b[l, 1][None]
        if l % 2 == 0:
            e = l // 2
            w_in = mix_w_in[e]
            o_z, o_xbc, o_dt = 0, SSD_INNER, SSD_INNER + SSD_XBC
            o_q = o_dt + 2 * SSD_HEADS
            w_main = jnp.concatenate([w_in[:, o_z:o_xbc], w_in[:, o_q:], w_in[:, o_xbc:o_dt]], axis=1).astype(BF16)
            w_dt = jnp.pad(w_in[:, o_dt:o_q], ((0, 0), (0, DT_PAD - 2 * SSD_HEADS))).astype(BF16)
            conv_w, conv_b = ssd_conv_w[e], ssd_conv_b[e][None]
            d_skip_x = jnp.repeat(ssd_d[e], SSD_HEAD_DIM)[None]
            w_out = mix_w_out[e].astype(BF16)
            lat = _in_proj(x, modl, False, w_main, w_dt, conv_w, conv_b)
            cx = _in_proj(xc, modl, True, w_main, w_dt, conv_w, conv_b)
            ys, os_ = {}, {}
            for dr in range(2):
                scan_args = dict(direction=dr, add_skip=dr == 0)
                zeros_ssd = jnp.zeros((bsz, SSD_STATE, SSD_INNER), F32)
                ys["c", dr], s_ctx = ssd_scan(cx[1], cx[2], cx[3], ssd_dt_bias[e, dr], ssd_a_log[e, dr], d_skip_x,
                                              zeros_ssd, **scan_args)
                ys["l", dr], _ = ssd_scan(lat[1], lat[2], lat[3], ssd_dt_bias[e, dr], ssd_a_log[e, dr], d_skip_x,
                                          s_ctx, **scan_args)
                zeros_hg = jnp.zeros((bsz, HG_HEADS, HG_DV, HG_DK), F32)
                os_["c", dr], st_ctx = gla_scan(cx[0], hg_lb_raw, zeros_hg, direction=dr, first=e == 0, layer=e)
                os_["l", dr], _ = gla_scan(lat[0], hg_lb_raw, st_ctx, direction=dr, first=e == 0, layer=e)
            nws, nwh = ssd_norm_w[e][None], hg_norm_w[e][None]
            x1 = mix_out(ys["l", 0], ys["l", 1], os_["l", 0], os_["l", 1], lat[0], nws, nwh, w_out,
                         x, modl, False, ln_g0, ln_b0)
            if ctx_next:
                xc1 = mix_out(ys["c", 0], ys["c", 1], os_["c", 0], os_["c", 1], cx[0], nws, nwh, w_out,
                              xc, modl, True, ln_g0, ln_b0)
        else:
            o = l // 2
            w1 = conf_w1[o].astype(BF16)
            w2 = conf_w2[o].astype(BF16)
            conv_args = (conf_dw_w[o], conf_dw_b[o][None], conf_ln_g[o][None], conf_ln_b[o][None])

            def conformer(t, is_ctx):
                glu = mod_matmul(t, modl, is_ctx, 0, 1, w1, n_out=d, glu=True, bias=conf_b1[o][None], name="conf_glu")
                u = conf_conv(glu, *conv_args)
                return matmul_res_ln(u, w2, conf_b2[o][None], t, modl, is_ctx, 2, ln_g0, ln_b0, name="conf_out")

            x1 = conformer(x, False)
            if ctx_next:
                xc1 = conformer(xc, True)
        ffn_args = (ffn_w_up[l].astype(BF16), ffn_conv_w[l].reshape(9, D_FF), ffn_conv_b[l][None],
                    ffn_w_down[l].astype(BF16), ln_g1, ln_b1)
        x = _ffn_block(x1, modl, False, *ffn_args, rows, GRID_W)
        if ctx_next:
            xc = _ffn_block(xc1, modl, True, *ffn_args, 1, ctx_len)
    return x
```

```python
import functools

import jax
import jax.numpy as jnp
from jax import lax
from jax.experimental import pallas as pl
from jax.experimental.pallas import tpu as pltpu

F32 = jnp.float32
BF16 = jnp.bfloat16
HIGHEST = lax.Precision.HIGHEST

D_MODEL = 1024
DEPTH = 4
GRID_W = 64
SSD_HEADS = 16
SSD_HEAD_DIM = 64
SSD_INNER = SSD_HEADS * SSD_HEAD_DIM
SSD_GROUPS = 2
SSD_STATE = 128
SSD_BC = SSD_GROUPS * SSD_STATE
SSD_XBC = SSD_INNER + 2 * SSD_BC
SSD_CHUNK = 128
HG_HEADS = 8
HG_DK = 128
HG_DV = D_MODEL // HG_HEADS
HG_INNER = HG_HEADS * HG_DK
HG_CHUNK = 128
HG_SUB = 8
D_FF = 2816
CONF_KERNEL = 31
ALPHA = (2 * DEPTH) ** 0.25
EPS = 1e-5
LOG2E = 1.4426950408889634
N_MOD = 6
MOD_ROWS = 8
DT_PAD = 128
LANES = 128
SUBLANES = 8
VMEM_LIMIT = 48 * 1024 * 1024
ACT = BF16
PACKED_ROWS = 16
MAX_TN = 1536

COL_Z, COL_Q, COL_F, COL_V, COL_G = 0, 1, 2, 4, 5
COL_XBC = 6 * D_MODEL
IN_MAIN = COL_XBC + SSD_XBC


def _params(*sem):
    return pltpu.CompilerParams(dimension_semantics=sem, vmem_limit_bytes=VMEM_LIMIT)


def _silu(t):
    return t * jax.nn.sigmoid(t)


def _softplus(t):
    return jnp.maximum(t, 0.0) + jnp.log1p(jnp.exp(-jnp.abs(t)))


def _split_dot(t, m, passes):
    acc, rem = None, t
    for i in range(passes):
        piece = rem.astype(BF16)
        part = jnp.dot(piece, m, preferred_element_type=F32)
        acc = part if acc is None else acc + part
        if i + 1 < passes:
            rem = rem - piece.astype(F32)
    return acc


def _res_ln(x, gate, y, g, b):
    r = ALPHA * x + gate * y
    mu = jnp.mean(r, axis=-1, keepdims=True)
    rc = r - mu
    var = jnp.mean(rc * rc, axis=-1, keepdims=True)
    return rc * lax.rsqrt(var + EPS) * g + b


def _mod_spec(k, ctx):
    if ctx:
        return pl.BlockSpec((1, 1, D_MODEL), lambda b, *_: ((MOD_ROWS // 2) * N_MOD + k, 0, 0))
    return pl.BlockSpec((1, 1, D_MODEL), lambda b, *_: (b * N_MOD + k, 0, 0))


def _ada_kernel(s_ref, w_ref, b_ref, o_ref):
    s = _silu(s_ref[...])
    o_ref[0] = jnp.dot(s, w_ref[0], precision=HIGHEST, preferred_element_type=F32) + b_ref[0]


def ada_table(s, ada_w, ada_b):
    depth, d, n = ada_w.shape
    tn = n // 4
    return pl.pallas_call(
        _ada_kernel,
        grid=(depth, n // tn),
        in_specs=[pl.BlockSpec((MOD_ROWS, d), lambda l, j: (0, 0)),
                  pl.BlockSpec((1, d, tn), lambda l, j: (l, 0, j)),
                  pl.BlockSpec((1, 1, tn), lambda l, j: (l, 0, j))],
        out_specs=pl.BlockSpec((1, MOD_ROWS, tn), lambda l, j: (l, 0, j)),
        out_shape=jax.ShapeDtypeStruct((depth, MOD_ROWS, n), F32),
        compiler_params=_params("parallel", "parallel"),
        name="ada_table",
    )(s, ada_w, ada_b.reshape(depth, 1, n))


def _mm_kernel(*refs, glu, has_bias, has_extra):
    it = iter(refs)
    x_ref, sh_ref, sc_ref, w_ref = next(it), next(it), next(it), next(it)
    wg_ref = next(it) if glu else None
    b_ref = next(it) if has_bias else None
    bg_ref = next(it) if (has_bias and glu) else None
    wx_ref = next(it) if has_extra else None
    o_ref = next(it)
    ox_ref = next(it) if has_extra else None
    xm_ref = next(it)

    @pl.when(pl.program_id(2) == 0)
    def _():
        xm = x_ref[0] * (1.0 + sc_ref[0]) + sh_ref[0]
        xm_ref[...] = xm.astype(BF16)
        if has_extra:
            ox_ref[0] = jnp.dot(xm_ref[...], wx_ref[...], preferred_element_type=F32)

    xm = xm_ref[...]
    acc = jnp.dot(xm, w_ref[...], preferred_element_type=F32)
    if has_bias:
        acc = acc + b_ref[...]
    if glu:
        gate = jnp.dot(xm, wg_ref[...], preferred_element_type=F32)
        if has_bias:
            gate = gate + bg_ref[...]
        acc = acc * jax.nn.sigmoid(gate)
    o_ref[0] = acc.astype(o_ref.dtype)


def mod_matmul(x, modl, ctx, k_shift, k_scale, w, *, n_out, glu=False, bias=None, w_extra=None, name):
    bsz, seq, d = x.shape
    tm = min(seq, 1024)
    tn = max(t for t in range(LANES, MAX_TN + 1, LANES) if n_out % t == 0)
    nj = n_out // tn
    in_specs = [pl.BlockSpec((1, tm, d), lambda b, i, j: (b, i, 0)),
                _mod_spec(k_shift, ctx), _mod_spec(k_scale, ctx),
                pl.BlockSpec((d, tn), lambda b, i, j: (0, j))]
    args = [x, modl, modl, w]
    if glu:
        in_specs.append(pl.BlockSpec((d, tn), lambda b, i, j: (0, j + nj)))
        args.append(w)
    if bias is not None:
        in_specs.append(pl.BlockSpec((1, tn), lambda b, i, j: (0, j)))
        args.append(bias)
        if glu:
            in_specs.append(pl.BlockSpec((1, tn), lambda b, i, j: (0, j + nj)))
            args.append(bias)
    out_shape = [jax.ShapeDtypeStruct((bsz, seq, n_out), ACT)]
    out_specs = [pl.BlockSpec((1, tm, tn), lambda b, i, j: (b, i, j))]
    if w_extra is not None:
        in_specs.append(pl.BlockSpec((d, DT_PAD), lambda b, i, j: (0, 0)))
        args.append(w_extra)
        out_shape.append(jax.ShapeDtypeStruct((bsz, seq, DT_PAD), F32))
        out_specs.append(pl.BlockSpec((1, tm, DT_PAD), lambda b, i, j: (b, i, 0)))
    res = pl.pallas_call(
        functools.partial(_mm_kernel, glu=glu, has_bias=bias is not None, has_extra=w_extra is not None),
        grid=(bsz, seq // tm, nj),
        in_specs=in_specs, out_specs=out_specs, out_shape=out_shape,
        scratch_shapes=[pltpu.VMEM((tm, d), BF16)],
        compiler_params=_params("parallel", "parallel", "arbitrary"),
        name=name,
    )(*args)
    return res if w_extra is not None else res[0]


def _resln_kernel(*refs, has_bias):
    it = iter(refs)
    a_ref, w_ref = next(it), next(it)
    b_ref = next(it) if has_bias else None
    x_ref, gate_ref, g_ref, beta_ref, o_ref = next(it), next(it), next(it), next(it), next(it)
    y = jnp.dot(a_ref[0], w_ref[...], preferred_element_type=F32)
    if has_bias:
        y = y + b_ref[...]
    o_ref[0] = _res_ln(x_ref[0], gate_ref[0], y, g_ref[...], beta_ref[...])


def matmul_res_ln(a, w, bias, x, modl, ctx, k_gate, ln_g, ln_b, *, name):
    bsz, seq, k = a.shape
    d = x.shape[-1]
    tm = min(seq, 512)
    row = lambda b, i: (b, i, 0)
    const = lambda b, i: (0, 0)
    in_specs = [pl.BlockSpec((1, tm, k), row), pl.BlockSpec((k, d), const)]
    args = [a, w]
    if bias is not None:
        in_specs.append(pl.BlockSpec((1, d), const))
        args.append(bias)
    in_specs += [pl.BlockSpec((1, tm, d), row), _mod_spec(k_gate, ctx),
                 pl.BlockSpec((1, d), const), pl.BlockSpec((1, d), const)]
    args += [x, modl, ln_g, ln_b]
    return pl.pallas_call(
        functools.partial(_resln_kernel, has_bias=bias is not None),
        grid=(bsz, seq // tm),
        in_specs=in_specs,
        out_specs=pl.BlockSpec((1, tm, d), row),
        out_shape=jax.ShapeDtypeStruct((bsz, seq, d), F32),
        compiler_params=_params("parallel", "parallel"),
        name=name,
    )(*args)


def _fill_halo(buf_ref, cur_ref, prev_ref, next_ref, halo, tl):
    i = pl.program_id(1)
    last = pl.num_programs(1) - 1
    buf_ref[0:halo] = jnp.where(i > 0, prev_ref[0].astype(F32), 0.0)
    buf_ref[halo:halo + tl] = cur_ref[0].astype(F32)
    buf_ref[halo + tl:halo + tl + halo] = jnp.where(i < last, next_ref[0].astype(F32), 0.0)


def _halo_specs(seq, tl, halo, width, col_block):
    per = tl // halo
    last = seq // halo - 1
    def prev_map(b, i, *rest):
        return (b, jnp.maximum(i * per - 1, 0), col_block(*rest))
    def next_map(b, i, *rest):
        return (b, jnp.minimum((i + 1) * per, last), col_block(*rest))
    return pl.BlockSpec((1, halo, width), prev_map), pl.BlockSpec((1, halo, width), next_map)


def _xbc_conv_kernel(cur_ref, prev_ref, next_ref, w_ref, b_ref, o_ref, buf_ref, *, taps, halo, tl):
    _fill_halo(buf_ref, cur_ref, prev_ref, next_ref, halo, tl)
    acc = jnp.broadcast_to(b_ref[...], (tl, b_ref.shape[-1]))
    for k in range(taps):
        off = halo - taps // 2 + k
        acc = acc + w_ref[k:k + 1, :] * buf_ref[off:off + tl, :]
    o_ref[0] = _silu(acc).astype(o_ref.dtype)


def xbc_conv(proj, conv_w, conv_b):
    bsz, seq, _ = proj.shape
    taps, width = conv_w.shape
    tc = 512
    tl = min(seq, 512)
    halo = PACKED_ROWS
    first = COL_XBC // tc
    prev_spec, next_spec = _halo_specs(seq, tl, halo, tc, lambda j: first + j)
    return pl.pallas_call(
        functools.partial(_xbc_conv_kernel, taps=taps, halo=halo, tl=tl),
        grid=(bsz, seq // tl, width // tc),
        in_specs=[pl.BlockSpec((1, tl, tc), lambda b, i, j: (b, i, first + j)), prev_spec, next_spec,
                  pl.BlockSpec((taps, tc), lambda b, i, j: (0, j)),
                  pl.BlockSpec((1, tc), lambda b, i, j: (0, j))],
        out_specs=pl.BlockSpec((1, tl, tc), lambda b, i, j: (b, i, j)),
        out_shape=jax.ShapeDtypeStruct((bsz, seq, width), ACT),
        scratch_shapes=[pltpu.VMEM((tl + 2 * halo, tc), F32)],
        compiler_params=_params("parallel", "parallel", "parallel"),
        name="xbc_conv",
    )(proj, proj, proj, conv_w, conv_b)


def _conf_conv_kernel(cur_ref, prev_ref, next_ref, w_ref, b_ref, g_ref, beta_ref, o_ref, buf_ref, ph_ref,
                      *, taps, halo, tl):
    _fill_halo(buf_ref, cur_ref, prev_ref, next_ref, halo, tl)
    rows = tl + 2 * halo
    full = buf_ref[...]
    for p in range(1, SUBLANES):
        ph_ref[p - 1] = pltpu.roll(full, rows - p, axis=0)
    acc = jnp.broadcast_to(b_ref[...], (tl, b_ref.shape[-1]))
    for k in range(taps):
        off = halo - taps // 2 + k
        p = off % SUBLANES
        base = off - p
        win = buf_ref[base:base + tl, :] if p == 0 else ph_ref[p - 1, base:base + tl, :]
        acc = acc + w_ref[k:k + 1, :] * win
    mu = jnp.mean(acc, axis=-1, keepdims=True)
    ac = acc - mu
    var = jnp.mean(ac * ac, axis=-1, keepdims=True)
    u = ac * lax.rsqrt(var + EPS) * g_ref[...] + beta_ref[...]
    o_ref[0] = _silu(u).astype(o_ref.dtype)


def conf_conv(glu, dw_w, dw_b, ln_g, ln_b):
    bsz, seq, d = glu.shape
    taps = dw_w.shape[0]
    tl = min(seq, 256)
    halo = PACKED_ROWS
    prev_spec, next_spec = _halo_specs(seq, tl, halo, d, lambda: 0)
    const = lambda b, i: (0, 0)
    return pl.pallas_call(
        functools.partial(_conf_conv_kernel, taps=taps, halo=halo, tl=tl),
        grid=(bsz, seq // tl),
        in_specs=[pl.BlockSpec((1, tl, d), lambda b, i: (b, i, 0)), prev_spec, next_spec,
                  pl.BlockSpec((taps, d), const), pl.BlockSpec((1, d), const),
                  pl.BlockSpec((1, d), const), pl.BlockSpec((1, d), const)],
        out_specs=pl.BlockSpec((1, tl, d), lambda b, i: (b, i, 0)),
        out_shape=jax.ShapeDtypeStruct((bsz, seq, d), BF16),
        scratch_shapes=[pltpu.VMEM((tl + 2 * halo, d), F32),
                        pltpu.VMEM((SUBLANES - 1, tl + 2 * halo, d), F32)],
        compiler_params=_params("parallel", "parallel"),
        name="conf_conv",
    )(glu, glu, glu, dw_w, dw_b, ln_g, ln_b)


def _ssd_kernel(xs_ref, b_ref, c_ref, dt_ref, dtt_ref, dtb_ref, dtbt_ref, alog_ref, alogt_ref, dsk_ref, s0_ref,
                *rest, rev, direction, add_skip, need_out):
    y_ref, sf_ref, st_ref = rest if need_out else (None,) + rest
    T = SSD_CHUNK
    H, P, N = SSD_HEADS, SSD_HEAD_DIM, SSD_STATE
    GW = SSD_INNER // SSD_GROUPS

    @pl.when(pl.program_id(1) == 0)
    def _():
        st_ref[...] = s0_ref[0]

    dt = _softplus(dt_ref[0][:, direction * H:(direction + 1) * H] + dtb_ref[...])
    a_col = dt * (-jnp.exp(alog_ref[...]))
    dt_row = _softplus(dtt_ref[0] + dtbt_ref[...])
    a_row = dt_row * (-jnp.exp(alogt_ref[...]))

    ti = lax.broadcasted_iota(jnp.int32, (T, T), 0)
    si = lax.broadcasted_iota(jnp.int32, (T, T), 1)
    valid = (si >= ti) if rev else (si <= ti)
    valid_f = valid.astype(F32)
    valid_t = ((ti >= si) if rev else (ti <= si)).astype(F32)
    acum = jnp.dot(valid_f, a_col, precision=HIGHEST, preferred_element_type=F32)
    acum_row = jnp.dot(a_row, valid_t, precision=HIGHEST, preferred_element_type=F32)
    atot = acum[0:1] if rev else acum[T - 1:T]

    e_p = (lax.broadcasted_iota(jnp.int32, (H, H * P), 1) // P
           == lax.broadcasted_iota(jnp.int32, (H, H * P), 0)).astype(BF16)
    expand = lambda t, passes: _split_dot(t, e_p, passes)
    eac_x = expand(jnp.exp(acum), 3) if need_out else None
    dtdec_x = expand(dt * jnp.exp(atot - acum), 2)
    eat_x = expand(jnp.broadcast_to(jnp.exp(atot), (SUBLANES, H)), 3)[0:1]

    xs = xs_ref[0].astype(F32)
    xdec = (xs * dtdec_x).astype(BF16)
    bm = b_ref[0].astype(BF16)
    cm = c_ref[0].astype(BF16)
    lane = lax.broadcasted_iota(jnp.int32, (T, 2 * P), 1)

    cbs, y_offs = [], []
    for g in range(SSD_GROUPS):
        bg = bm[:, g * N:(g + 1) * N]
        cg = cm[:, g * N:(g + 1) * N]
        s_g = st_ref[:, g * GW:(g + 1) * GW]
        if need_out:
            cbs.append(lax.dot_general(cg, bg, (((1,), (1,)), ((), ())), preferred_element_type=F32))
            y_offs.append(jnp.dot(cg, s_g.astype(BF16), preferred_element_type=F32))
        upd = lax.dot_general(bg, xdec[:, g * GW:(g + 1) * GW], (((0,), (0,)), ((), ())),
                              preferred_element_type=F32)
        st_ref[:, g * GW:(g + 1) * GW] = s_g * eat_x[:, g * GW:(g + 1) * GW] + upd
    pairs = []
    for pp in range(H // 2 if need_out else 0):
        l_pair = []
        for h in (2 * pp, 2 * pp + 1):
            diff = acum[:, h:h + 1] - acum_row[h:h + 1, :]
            decay = jnp.where(valid, jnp.exp(jnp.minimum(diff, 0.0)), 0.0)
            l_pair.append(decay * (cbs[h // (H // SSD_GROUPS)] * dt_row[h:h + 1, :]))
        lhs = jnp.concatenate(l_pair, axis=1).astype(BF16)
        xp = xs[:, 2 * pp * P:(2 * pp + 2) * P]
        rhs = jnp.concatenate([jnp.where(lane < P, xp, 0.0), jnp.where(lane >= P, xp, 0.0)],
                              axis=0).astype(BF16)
        pairs.append((lhs, rhs))
    if need_out:
        y_diag = jnp.concatenate([jnp.dot(lhs, rhs, preferred_element_type=F32) for lhs, rhs in pairs], axis=1)
        y = y_diag + jnp.concatenate(y_offs, axis=1) * eac_x
        if add_skip:
            y = y + dsk_ref[...] * xs
        y_ref[0] = y.astype(y_ref.dtype)

    @pl.when(pl.program_id(1) == pl.num_programs(1) - 1)
    def _():
        sf_ref[0] = st_ref[...]


def ssd_scan(xbc, dt, dt_t, dt_bias, a_log, d_skip_x, s0, *, direction, add_skip, need_out=True):
    bsz, seq, _ = xbc.shape
    T, H = SSD_CHUNK, SSD_HEADS
    nc = seq // T
    rev = direction == 1
    chunk = (lambda c: nc - 1 - c) if rev else (lambda c: c)
    const = lambda b, c: (0, 0)
    y_spec = [pl.BlockSpec((1, T, SSD_INNER), lambda b, c: (b, chunk(c), 0))] if need_out else []
    y_shape = [jax.ShapeDtypeStruct((bsz, seq, SSD_INNER), ACT)] if need_out else []
    res = pl.pallas_call(
        functools.partial(_ssd_kernel, rev=rev, direction=direction, add_skip=add_skip, need_out=need_out),
        grid=(bsz, nc),
        in_specs=[pl.BlockSpec((1, T, SSD_INNER), lambda b, c: (b, chunk(c), 0)),
                  pl.BlockSpec((1, T, SSD_BC), lambda b, c: (b, chunk(c), SSD_INNER // SSD_BC)),
                  pl.BlockSpec((1, T, SSD_BC), lambda b, c: (b, chunk(c), SSD_INNER // SSD_BC + 1)),
                  pl.BlockSpec((1, T, DT_PAD), lambda b, c: (b, chunk(c), 0)),
                  pl.BlockSpec((1, H, T), lambda b, c: (b, direction, chunk(c))),
                  pl.BlockSpec((1, H), const), pl.BlockSpec((H, 1), const),
                  pl.BlockSpec((1, H), const), pl.BlockSpec((H, 1), const),
                  pl.BlockSpec((1, SSD_INNER), const),
                  pl.BlockSpec((1, SSD_STATE, SSD_INNER), lambda b, c: (b, 0, 0))],
        out_specs=y_spec + [pl.BlockSpec((1, SSD_STATE, SSD_INNER), lambda b, c: (b, 0, 0))],
        out_shape=y_shape + [jax.ShapeDtypeStruct((bsz, SSD_STATE, SSD_INNER), F32)],
        scratch_shapes=[pltpu.VMEM((SSD_STATE, SSD_INNER), F32)],
        compiler_params=_params("parallel", "arbitrary"),
        name=f"ssd_scan_d{direction}",
    )(xbc, xbc, xbc, dt, dt_t, dt_bias[None, :], dt_bias[:, None], a_log[None, :], a_log[:, None], d_skip_x, s0)
    return (res[0], res[1]) if need_out else (None, res[0])


def _gla_kernel(q_ref, f_ref, v_ref, lbraw_ref, s0_ref, *rest, rev, first, layer, need_out):
    o_ref, sf_ref, st_ref, c_scr = rest if need_out else (None,) + rest
    T, SB = HG_CHUNK, HG_SUB
    H, DK, DV = HG_HEADS, HG_DK, HG_DV
    W = H * DK
    NT = (((1,), (1,)), ((), ()))
    TN = (((0,), (0,)), ((), ()))

    @pl.when(pl.program_id(1) == 0)
    def _():
        st_ref[...] = s0_ref[0]

    a2 = f_ref[0].astype(F32) * LOG2E
    log_sig = jnp.minimum(a2, 0.0) - jnp.log(1.0 + jnp.exp2(-jnp.abs(a2))) * LOG2E
    log_sig_neg = log_sig - a2
    if first:
        lf, log_k = log_sig, log_sig_neg
    else:
        raw = lbraw_ref[...]
        pr = jnp.exp(raw - jnp.max(raw, axis=0, keepdims=True))
        pr = pr / jnp.sum(pr, axis=0, keepdims=True)
        lb = jnp.sum(pr[0:layer + 1], axis=0, keepdims=True) - pr[0:1]
        lf = jnp.log(lb + (1.0 - lb) * jnp.exp2(log_sig)) * LOG2E
        log_k = jnp.log(1.0 - lb) * LOG2E + log_sig_neg
    kk = jnp.exp2(log_k)
    q = _silu(q_ref[0].astype(F32))
    v_bf = v_ref[0].astype(BF16)

    ti = lax.broadcasted_iota(jnp.int32, (T, T), 0)
    si = lax.broadcasted_iota(jnp.int32, (T, T), 1)
    valid_f = ((si >= ti) if rev else (si <= ti)).astype(F32)
    acum = jnp.dot(valid_f, lf, precision=HIGHEST, preferred_element_type=F32)
    atot = acum[0:1] if rev else acum[T - 1:T]
    kd = (kk * jnp.exp2(atot - acum)).astype(BF16)
    head = lambda t, h: t[:, h * DK:(h + 1) * DK]

    def write_final_state():
        @pl.when(pl.program_id(1) == pl.num_programs(1) - 1)
        def _():
            sf_ref[0] = st_ref[...]

    if not need_out:
        for h in range(H):
            upd_h = lax.dot_general(head(v_bf, h), head(kd, h), TN, preferred_element_type=F32)
            st_ref[h] = st_ref[h] * jnp.exp2(head(atot, h)) + upd_h
        write_final_state()
        return

    qe = (q * jnp.exp2(acum)).astype(BF16)
    c_src = acum - log_k
    for h in range(H):
        c_scr[h] = head(c_src, h)

    level_p = []
    b = T // 2
    while b >= SB:
        q_parts, k_parts = [], []
        zeros = jnp.zeros((b, W), F32)
        for c0 in range(0, T, 2 * b):
            lo, hi = slice(c0, c0 + b), slice(c0 + b, c0 + 2 * b)
            src, tgt = (hi, lo) if rev else (lo, hi)
            ref_row = c0 + b if rev else c0 + b - 1
            r = acum[ref_row:ref_row + 1]
            q_t = q[tgt] * jnp.exp2(acum[tgt] - r)
            k_s = kk[src] * jnp.exp2(r - acum[src])
            q_parts += [q_t, zeros] if rev else [zeros, q_t]
            k_parts += [zeros, k_s] if rev else [k_s, zeros]
        q_l = jnp.concatenate(q_parts, axis=0).astype(BF16)
        k_l = jnp.concatenate(k_parts, axis=0).astype(BF16)
        level_p.append([lax.dot_general(head(q_l, h), head(k_l, h), NT, preferred_element_type=F32)
                        for h in range(H)])
        b //= 2

    inter = [lax.dot_general(head(qe, h), st_ref[h].astype(BF16), NT, preferred_element_type=F32) for h in range(H)]
    upd = [lax.dot_general(head(v_bf, h), head(kd, h), TN, preferred_element_type=F32) for h in range(H)]
    for h in range(H):
        st_ref[h] = st_ref[h] * jnp.exp2(head(atot, h)) + upd[h]

    row = lax.broadcasted_iota(jnp.int32, (SB, T), 0)
    lane = lax.broadcasted_iota(jnp.int32, (SB, T), 1)
    diag_p = [[] for _ in range(H)]
    for lo in range(0, T, SB):
        a_i, q_i = acum[lo:lo + SB], q[lo:lo + SB]
        p_h = [jnp.zeros((SB, T), F32) for _ in range(H)]
        for s in range(SB):
            here = (lane == lo + s) & ((row <= s) if rev else (row >= s))
            for h in range(H):
                c_s = jnp.broadcast_to(c_scr[h, lo + s:lo + s + 1, :], (SB, DK))
                w = head(q_i, h) * jnp.exp2(head(a_i, h) - c_s)
                p_h[h] = jnp.where(here, jnp.sum(w, axis=-1, keepdims=True), p_h[h])
        for h in range(H):
            diag_p[h].append(p_h[h])

    same = []
    b = T // 4
    while b >= SB:
        shift = (2 * b).bit_length() - 1
        same.append((ti >> shift) == (si >> shift))
        b //= 2
    out = []
    for h in range(H):
        p = level_p[0][h]
        for li, m in enumerate(same):
            p = jnp.where(m, level_p[li + 1][h], p)
        p = p + jnp.concatenate(diag_p[h], axis=0)
        out.append(jnp.dot(p.astype(BF16), head(v_bf, h), preferred_element_type=F32) + inter[h])
    o_ref[0] = jnp.concatenate(out, axis=1).astype(o_ref.dtype)
    write_final_state()


def gla_scan(proj, lb_raw, s0, *, direction, first, layer, need_out=True):
    bsz, seq, _ = proj.shape
    T, H = HG_CHUNK, HG_HEADS
    nc = seq // T
    rev = direction == 1
    chunk = (lambda c: nc - 1 - c) if rev else (lambda c: c)
    o_spec = [pl.BlockSpec((1, T, HG_INNER), lambda b, c: (b, chunk(c), 0))] if need_out else []
    o_shape = [jax.ShapeDtypeStruct((bsz, seq, HG_INNER), ACT)] if need_out else []
    res = pl.pallas_call(
        functools.partial(_gla_kernel, rev=rev, first=first, layer=layer, need_out=need_out),
        grid=(bsz, nc),
        in_specs=[pl.BlockSpec((1, T, HG_INNER), lambda b, c: (b, chunk(c), COL_Q)),
                  pl.BlockSpec((1, T, HG_INNER), lambda b, c: (b, chunk(c), COL_F + direction)),
                  pl.BlockSpec((1, T, HG_INNER), lambda b, c: (b, chunk(c), COL_V)),
                  pl.BlockSpec(lb_raw.shape, lambda b, c: (0, 0)),
                  pl.BlockSpec((1, H, HG_DV, HG_DK), lambda b, c: (b, 0, 0, 0))],
        out_specs=o_spec + [pl.BlockSpec((1, H, HG_DV, HG_DK), lambda b, c: (b, 0, 0, 0))],
        out_shape=o_shape + [jax.ShapeDtypeStruct((bsz, H, HG_DV, HG_DK), F32)],
        scratch_shapes=[pltpu.VMEM((H, HG_DV, HG_DK), F32),
                        pltpu.VMEM((H, T, HG_DK), F32)],
        compiler_params=_params("parallel", "arbitrary"),
        name=f"gla_scan_d{direction}",
    )(proj, proj, proj, lb_raw, s0)
    return (res[0], res[1]) if need_out else (None, res[0])


def _mixout_kernel(yf_ref, yb_ref, z_ref, of_ref, ob_ref, g_ref, nws_ref, nwh_ref, w_ref,
                   x_ref, gate_ref, lg_ref, lb_ref, o_ref):
    def group_rms(t, groups):
        width = t.shape[-1] // groups
        parts = []
        for gi in range(groups):
            tg = t[:, gi * width:(gi + 1) * width]
            ms = jnp.mean(tg * tg, axis=-1, keepdims=True)
            parts.append(tg * lax.rsqrt(ms + EPS))
        return jnp.concatenate(parts, axis=1)

    f32 = lambda r: r[0].astype(F32)
    ys = group_rms((f32(yf_ref) + f32(yb_ref)) * _silu(f32(z_ref)), SSD_GROUPS) * nws_ref[...]
    os_ = group_rms(f32(of_ref) + f32(ob_ref), HG_HEADS) * nwh_ref[...] * _silu(f32(g_ref))
    a = jnp.concatenate([ys, os_], axis=1).astype(BF16)
    y = jnp.dot(a, w_ref[...], preferred_element_type=F32)
    o_ref[0] = _res_ln(x_ref[0], gate_ref[0], y, lg_ref[...], lb_ref[...])


def mix_out(y_f, y_b, o_f, o_b, proj, ssd_nw, hg_nw, w_out, x, modl, ctx, ln_g, ln_b):
    bsz, seq, d = x.shape
    tm = min(seq, 256)
    row = lambda b, i: (b, i, 0)
    const = lambda b, i: (0, 0)
    act = pl.BlockSpec((1, tm, d), row)
    return pl.pallas_call(
        _mixout_kernel,
        grid=(bsz, seq // tm),
        in_specs=[act, act, pl.BlockSpec((1, tm, d), lambda b, i: (b, i, COL_Z)), act, act,
                  pl.BlockSpec((1, tm, d), lambda b, i: (b, i, COL_G)),
                  pl.BlockSpec((1, d), const), pl.BlockSpec((1, d), const),
                  pl.BlockSpec(w_out.shape, const), act, _mod_spec(2, ctx),
                  pl.BlockSpec((1, d), const), pl.BlockSpec((1, d), const)],
        out_specs=act,
        out_shape=jax.ShapeDtypeStruct((bsz, seq, d), F32),
        compiler_params=_params("parallel", "parallel"),
        name="mix_out",
    )(y_f, y_b, proj, o_f, o_b, proj, ssd_nw, hg_nw, w_out, x, modl, ln_g, ln_b)


def _ffn_kernel(x_ref, xp_ref, xn_ref, sh_ref, sc_ref, gate_ref, wv0_ref, wg0_ref, wv_ref, wg_ref, cw_ref, cb_ref,
                wd_ref, lg_ref, lb_ref, o_ref, xm_ref, g_a, g_b, v_a, v_b, c0_ref, c2_ref, acc_ref,
                *, rows, cols, tl, halo, rb, n_tiles):
    i, j = pl.program_id(1), pl.program_id(2)
    last_i, last_j = pl.num_programs(1) - 1, pl.num_programs(2) - 1
    pad = SUBLANES
    ext = tl + 2 * halo
    tc = wd_ref.shape[0]

    n_parts = tl // rb

    def up_project(wv, wg, g_dst, v_dst, part=None):
        parts = range(n_parts) if part is None else (part,)
        for k in parts:
            v_lo, g_lo, g_n = k * rb, k * (ext // n_parts), ext // n_parts
            v_dst[v_lo:v_lo + rb, :] = jnp.dot(xm_ref[halo + v_lo:halo + v_lo + rb, :], wv[...],
                                               preferred_element_type=F32)
            g_dst[pad + g_lo:pad + g_lo + g_n, :] = jnp.dot(xm_ref[g_lo:g_lo + g_n, :], wg[...],
                                                             preferred_element_type=F32)
        if halo and 0 in parts:
            g_dst[pad:pad + halo, :] = jnp.where(i > 0, g_dst[pad:pad + halo, :], 0.0)
        if halo and n_parts - 1 in parts:
            top = pad + halo + tl
            g_dst[top:top + halo, :] = jnp.where(i < last_i, g_dst[top:top + halo, :], 0.0)

    @pl.when(j == 0)
    def _():
        scale, shift = 1.0 + sc_ref[0], sh_ref[0]
        xm_ref[halo:halo + tl, :] = (x_ref[0] * scale + shift).astype(BF16)
        if halo:
            xm_ref[0:halo, :] = (xp_ref[0] * scale + shift).astype(BF16)
            xm_ref[halo + tl:ext, :] = (xn_ref[0] * scale + shift).astype(BF16)
        for g_buf in (g_a, g_b):
            g_buf[0:pad, :] = jnp.zeros((pad, tc), F32)
            g_buf[pad + ext:pad + ext + pad, :] = jnp.zeros((pad, tc), F32)
        acc_ref[...] = jnp.zeros_like(acc_ref)
        up_project(wv0_ref, wg0_ref, g_a, v_a)

    def step(g_cur, v_cur, g_nxt, v_nxt, with_up=True):
        n = rb + 2 * pad
        for t0 in range(0, tl, rb):
            if with_up:
                up_project(wv_ref, wg_ref, g_nxt, v_nxt, part=t0 // rb)
            e0 = halo + t0
            mid = g_cur[e0:e0 + n, :]
            col_sums = []
            for k in range(3):
                c = cw_ref[3 + k:4 + k, :] * mid
                if rows > 1:
                    c = c + cw_ref[k:k + 1, :] * g_cur[e0 - cols:e0 - cols + n, :]
                    c = c + cw_ref[6 + k:7 + k, :] * g_cur[e0 + cols:e0 + cols + n, :]
                col_sums.append(c)
            c0_ref[...] = col_sums[0]
            c2_ref[...] = col_sums[2]
            col = (lax.broadcasted_iota(jnp.int32, (rb, 1), 0) + t0) & (cols - 1)
            u = (col_sums[1][pad:pad + rb] + cb_ref[...]
                 + jnp.where(col > 0, c0_ref[pad - 1:pad - 1 + rb, :], 0.0)
                 + jnp.where(col < cols - 1, c2_ref[pad + 1:pad + 1 + rb, :], 0.0))
            act = (u * (1.0 + lax.erf(u * (2.0 ** -0.5)))) * v_cur[t0:t0 + rb, :]
            acc_ref[t0:t0 + rb, :] += jnp.dot(act.astype(BF16), wd_ref[...], preferred_element_type=F32)

    @pl.when(jnp.logical_and(j % 2 == 0, j < last_j))
    def _():
        step(g_a, v_a, g_b, v_b)

    @pl.when(jnp.logical_and(j % 2 == 1, j < last_j))
    def _():
        step(g_b, v_b, g_a, v_a)

    @pl.when(j == last_j)
    def _():
        if (n_tiles - 1) % 2 == 0:
            step(g_a, v_a, g_b, v_b, with_up=False)
        else:
            step(g_b, v_b, g_a, v_a, with_up=False)
        o_ref[0] = _res_ln(x_ref[0], gate_ref[0], acc_ref[...], lg_ref[...], lb_ref[...])


def _ffn_block(x, modl, ctx, w_up, conv_w, conv_b, w_down, ln_g, ln_b, rows, cols):
    bsz, seq, d = x.shape
    assert seq == rows * cols and cols & (cols - 1) == 0
    tc = 256
    nj = D_FF // tc
    tl = min(seq, 1024)
    assert tl % cols == 0
    halo = LANES if rows > 1 else 0
    assert halo == 0 or (halo > cols and tl % halo == 0)
    hb = halo if halo else SUBLANES
    per = tl // hb
    last_blk = seq // hb - 1
    rb = min(tl, 256)
    ext = tl + 2 * halo
    const = lambda b, i, j: (0, 0)
    nxt = lambda j: jnp.minimum(j + 1, nj - 1)
    return pl.pallas_call(
        functools.partial(_ffn_kernel, rows=rows, cols=cols, tl=tl, halo=halo, rb=rb, n_tiles=nj),
        grid=(bsz, seq // tl, nj),
        in_specs=[pl.BlockSpec((1, tl, d), lambda b, i, j: (b, i, 0)),
                  pl.BlockSpec((1, hb, d), lambda b, i, j: (b, jnp.maximum(i * per - 1, 0), 0)),
                  pl.BlockSpec((1, hb, d), lambda b, i, j: (b, jnp.minimum((i + 1) * per, last_blk), 0)),
                  _mod_spec(3, ctx), _mod_spec(4, ctx), _mod_spec(5, ctx),
                  pl.BlockSpec((d, tc), lambda b, i, j: (0, 0)),
                  pl.BlockSpec((d, tc), lambda b, i, j: (0, nj)),
                  pl.BlockSpec((d, tc), lambda b, i, j: (0, nxt(j))),
                  pl.BlockSpec((d, tc), lambda b, i, j: (0, nj + nxt(j))),
                  pl.BlockSpec((9, tc), lambda b, i, j: (0, j)),
                  pl.BlockSpec((1, tc), lambda b, i, j: (0, j)),
                  pl.BlockSpec((tc, d), lambda b, i, j: (j, 0)),
                  pl.BlockSpec((1, d), const), pl.BlockSpec((1, d), const)],
        out_specs=pl.BlockSpec((1, tl, d), lambda b, i, j: (b, i, 0)),
        out_shape=jax.ShapeDtypeStruct((bsz, seq, d), F32),
        scratch_shapes=[pltpu.VMEM((ext, d), BF16),
                        pltpu.VMEM((ext + 2 * SUBLANES, tc), F32), pltpu.VMEM((ext + 2 * SUBLANES, tc), F32),
                        pltpu.VMEM((tl, tc), F32), pltpu.VMEM((tl, tc), F32),
                        pltpu.VMEM((rb + 2 * SUBLANES, tc), F32),
                        pltpu.VMEM((rb + 2 * SUBLANES, tc), F32),
                        pltpu.VMEM((tl, d), F32)],
        compiler_params=_params("parallel", "parallel", "arbitrary"),
        name="ffn",
    )(x, x, x, modl, modl, modl, w_up, w_up, w_up, w_up, conv_w, conv_b, w_down, ln_g, ln_b)


def _in_proj(x, modl, ctx, w_main, w_dt, conv_w, conv_b):
    proj, dt = mod_matmul(x, modl, ctx, 0, 1, w_main, n_out=IN_MAIN, w_extra=w_dt, name="in_proj")
    xbc = xbc_conv(proj, conv_w, conv_b)
    dt_t = jnp.swapaxes(dt[:, :, :2 * SSD_HEADS], 1, 2)
    return proj, xbc, dt, dt_t


def kernel(x, c, ctx, c_ctx, ada_w, ada_b, post_ln_g, post_ln_b, mix_w_in, ssd_conv_w, ssd_conv_b, ssd_dt_bias,
           ssd_a_log, ssd_d, ssd_norm_w, hg_lb_raw, hg_norm_w, mix_w_out, conf_w1, conf_b1, conf_dw_w, conf_dw_b,
           conf_ln_g, conf_ln_b, conf_w2, conf_b2, ffn_w_up, ffn_conv_w, ffn_conv_b, ffn_w_down):
    bsz, seq, d = x.shape
    rows = seq // GRID_W
    ctx_len = ctx.shape[1]
    assert d == D_MODEL and bsz <= MOD_ROWS // 2

    cond = jnp.zeros((MOD_ROWS, d), F32).at[:bsz].set(c).at[MOD_ROWS // 2].set(c_ctx)
    mod = ada_table(cond, ada_w, ada_b).reshape(DEPTH, MOD_ROWS * N_MOD, 1, d)

    xc = ctx
    for l in range(DEPTH):
        modl = mod[l]
        ctx_next = any(j % 2 == 0 for j in range(l + 1, DEPTH))
        ln_g0, ln_b0 = post_ln_g[l, 0][None], post_ln_b[l, 0][None]
        ln_g1, ln_b1 = post_ln_g[l, 1][None], post_ln_b[l, 1][None]
        if l % 2 == 0:
            e = l // 2
            w_in = mix_w_in[e]
            o_z, o_xbc, o_dt = 0, SSD_INNER, SSD_INNER + SSD_XBC
            o_q = o_dt + 2 * SSD_HEADS
            w_main = jnp.concatenate([w_in[:, o_z:o_xbc], w_in[:, o_q:], w_in[:, o_xbc:o_dt]], axis=1).astype(BF16)
            w_dt = jnp.pad(w_in[:, o_dt:o_q], ((0, 0), (0, DT_PAD - 2 * SSD_HEADS))).astype(BF16)
            conv_w, conv_b = ssd_conv_w[e], ssd_conv_b[e][None]
            d_skip_x = jnp.repeat(ssd_d[e], SSD_HEAD_DIM)[None]
            w_out = mix_w_out[e].astype(BF16)
            lat = _in_proj(x, modl, False, w_main, w_dt, conv_w, conv_b)
            cx = _in_proj(xc, modl, True, w_main, w_dt, conv_w, conv_b)
            ys, os_ = {}, {}
            for dr in range(2):
                scan_args = dict(direction=dr, add_skip=dr == 0)
                zeros_ssd = jnp.zeros((bsz, SSD_STATE, SSD_INNER), F32)
                ys["c", dr], s_ctx = ssd_scan(cx[1], cx[2], cx[3], ssd_dt_bias[e, dr], ssd_a_log[e, dr], d_skip_x,
                                              zeros_ssd, need_out=ctx_next, **scan_args)
                ys["l", dr], _ = ssd_scan(lat[1], lat[2], lat[3], ssd_dt_bias[e, dr], ssd_a_log[e, dr], d_skip_x,
                                          s_ctx, **scan_args)
                zeros_hg = jnp.zeros((bsz, HG_HEADS, HG_DV, HG_DK), F32)
                os_["c", dr], st_ctx = gla_scan(cx[0], hg_lb_raw, zeros_hg, direction=dr, first=e == 0, layer=e,
                                                need_out=ctx_next)
                os_["l", dr], _ = gla_scan(lat[0], hg_lb_raw, st_ctx, direction=dr, first=e == 0, layer=e)
            nws, nwh = ssd_norm_w[e][None], hg_norm_w[e][None]
            x1 = mix_out(ys["l", 0], ys["l", 1], os_["l", 0], os_["l", 1], lat[0], nws, nwh, w_out,
                         x, modl, False, ln_g0, ln_b0)
            if ctx_next:
                xc1 = mix_out(ys["c", 0], ys["c", 1], os_["c", 0], os_["c", 1], cx[0], nws, nwh, w_out,
                              xc, modl, True, ln_g0, ln_b0)
        else:
            o = l // 2
            w1 = conf_w1[o].astype(BF16)
            w2 = conf_w2[o].astype(BF16)
            conv_args = (conf_dw_w[o], conf_dw_b[o][None], conf_ln_g[o][None], conf_ln_b[o][None])

            def conformer(t, is_ctx):
                glu = mod_matmul(t, modl, is_ctx, 0, 1, w1, n_out=d, glu=True, bias=conf_b1[o][None], name="conf_glu")
                u = conf_conv(glu, *conv_args)
                return matmul_res_ln(u, w2, conf_b2[o][None], t, modl, is_ctx, 2, ln_g0, ln_b0, name="conf_out")

            x1 = conformer(x, False)
            if ctx_next:
                xc1 = conformer(xc, True)
        ffn_args = (ffn_w_up[l].astype(BF16), ffn_conv_w[l].reshape(9, D_FF), ffn_conv_b[l][None],
                    (0.5 * ffn_w_down[l]).astype(BF16), ln_g1, ln_b1)
        x = _ffn_block(x1, modl, False, *ffn_args, rows, GRID_W)
        if ctx_next:
            xc = _ffn_block(xc1, modl, True, *ffn_args, 1, ctx_len)
    return x
```

```python
import functools

import jax
import jax.numpy as jnp
from jax import lax
from jax.experimental import pallas as pl
from jax.experimental.pallas import tpu as pltpu

F32 = jnp.float32
BF16 = jnp.bfloat16
HIGHEST = lax.Precision.HIGHEST

D_MODEL = 1024
DEPTH = 4
GRID_W = 64
SSD_HEADS = 16
SSD_HEAD_DIM = 64
SSD_INNER = SSD_HEADS * SSD_HEAD_DIM
SSD_GROUPS = 2
SSD_STATE = 128
SSD_BC = SSD_GROUPS * SSD_STATE
SSD_XBC = SSD_INNER + 2 * SSD_BC
SSD_CHUNK = 128
HG_HEADS = 8
HG_DK = 128
HG_DV = D_MODEL // HG_HEADS
HG_INNER = HG_HEADS * HG_DK
HG_CHUNK = 128
HG_SUB = 8
SCAN_SUB = 2
D_FF = 2816
CONF_KERNEL = 31
ALPHA = (2 * DEPTH) ** 0.25
EPS = 1e-5
LOG2E = 1.4426950408889634
N_MOD = 6
MOD_ROWS = 8
DT_PAD = 128
LANES = 128
SUBLANES = 8
VMEM_LIMIT = 48 * 1024 * 1024
ACT = BF16
PACKED_ROWS = 16
MAX_TN = 1536

COL_Z, COL_Q, COL_F, COL_V, COL_G = 0, 1, 2, 4, 5
COL_XBC = 6 * D_MODEL
IN_MAIN = COL_XBC + SSD_XBC


def _params(*sem):
    return pltpu.CompilerParams(dimension_semantics=sem, vmem_limit_bytes=VMEM_LIMIT)


def _silu(t):
    return t * jax.nn.sigmoid(t)


def _softplus(t):
    return jnp.maximum(t, 0.0) + jnp.log1p(jnp.exp(-jnp.abs(t)))


def _split_dot(t, m, passes):
    acc, rem = None, t
    for i in range(passes):
        piece = rem.astype(BF16)
        part = jnp.dot(piece, m, preferred_element_type=F32)
        acc = part if acc is None else acc + part
        if i + 1 < passes:
            rem = rem - piece.astype(F32)
    return acc


def _res_ln(x, gate, y, g, b):
    r = ALPHA * x + gate * y
    mu = jnp.mean(r, axis=-1, keepdims=True)
    rc = r - mu
    var = jnp.mean(rc * rc, axis=-1, keepdims=True)
    return rc * lax.rsqrt(var + EPS) * g + b


def _mod_spec(k, ctx):
    if ctx:
        return pl.BlockSpec((1, 1, D_MODEL), lambda b, *_: ((MOD_ROWS // 2) * N_MOD + k, 0, 0))
    return pl.BlockSpec((1, 1, D_MODEL), lambda b, *_: (b * N_MOD + k, 0, 0))


def _ada_kernel(s_ref, w_ref, b_ref, o_ref):
    s = _silu(s_ref[...])
    o_ref[0] = jnp.dot(s, w_ref[0], precision=HIGHEST, preferred_element_type=F32) + b_ref[0]


def ada_table(s, ada_w, ada_b):
    depth, d, n = ada_w.shape
    tn = n // 4
    return pl.pallas_call(
        _ada_kernel,
        grid=(depth, n // tn),
        in_specs=[pl.BlockSpec((MOD_ROWS, d), lambda l, j: (0, 0)),
                  pl.BlockSpec((1, d, tn), lambda l, j: (l, 0, j)),
                  pl.BlockSpec((1, 1, tn), lambda l, j: (l, 0, j))],
        out_specs=pl.BlockSpec((1, MOD_ROWS, tn), lambda l, j: (l, 0, j)),
        out_shape=jax.ShapeDtypeStruct((depth, MOD_ROWS, n), F32),
        compiler_params=_params("parallel", "parallel"),
        name="ada_table",
    )(s, ada_w, ada_b.reshape(depth, 1, n))


def _mm_kernel(*refs, glu, has_bias, has_extra):
    it = iter(refs)
    x_ref, sh_ref, sc_ref, w_ref = next(it), next(it), next(it), next(it)
    wg_ref = next(it) if glu else None
    b_ref = next(it) if has_bias else None
    bg_ref = next(it) if (has_bias and glu) else None
    wx_ref = next(it) if has_extra else None
    o_ref = next(it)
    ox_ref = next(it) if has_extra else None
    xm_ref = next(it)

    @pl.when(pl.program_id(2) == 0)
    def _():
        xm = x_ref[0] * (1.0 + sc_ref[0]) + sh_ref[0]
        xm_ref[...] = xm.astype(BF16)
        if has_extra:
            ox_ref[0] = jnp.dot(xm_ref[...], wx_ref[...], preferred_element_type=F32)

    xm = xm_ref[...]
    acc = jnp.dot(xm, w_ref[...], preferred_element_type=F32)
    if has_bias:
        acc = acc + b_ref[...]
    if glu:
        gate = jnp.dot(xm, wg_ref[...], preferred_element_type=F32)
        if has_bias:
            gate = gate + bg_ref[...]
        acc = acc * jax.nn.sigmoid(gate)
    o_ref[0] = acc.astype(o_ref.dtype)


def mod_matmul(x, modl, ctx, k_shift, k_scale, w, *, n_out, glu=False, bias=None, w_extra=None, name):
    bsz, seq, d = x.shape
    tm = min(seq, 1024)
    tn = max(t for t in range(LANES, MAX_TN + 1, LANES) if n_out % t == 0)
    nj = n_out // tn
    in_specs = [pl.BlockSpec((1, tm, d), lambda b, i, j: (b, i, 0)),
                _mod_spec(k_shift, ctx), _mod_spec(k_scale, ctx),
                pl.BlockSpec((d, tn), lambda b, i, j: (0, j))]
    args = [x, modl, modl, w]
    if glu:
        in_specs.append(pl.BlockSpec((d, tn), lambda b, i, j: (0, j + nj)))
        args.append(w)
    if bias is not None:
        in_specs.append(pl.BlockSpec((1, tn), lambda b, i, j: (0, j)))
        args.append(bias)
        if glu:
            in_specs.append(pl.BlockSpec((1, tn), lambda b, i, j: (0, j + nj)))
            args.append(bias)
    out_shape = [jax.ShapeDtypeStruct((bsz, seq, n_out), ACT)]
    out_specs = [pl.BlockSpec((1, tm, tn), lambda b, i, j: (b, i, j))]
    if w_extra is not None:
        in_specs.append(pl.BlockSpec((d, DT_PAD), lambda b, i, j: (0, 0)))
        args.append(w_extra)
        out_shape.append(jax.ShapeDtypeStruct((bsz, seq, DT_PAD), F32))
        out_specs.append(pl.BlockSpec((1, tm, DT_PAD), lambda b, i, j: (b, i, 0)))
    res = pl.pallas_call(
        functools.partial(_mm_kernel, glu=glu, has_bias=bias is not None, has_extra=w_extra is not None),
        grid=(bsz, seq // tm, nj),
        in_specs=in_specs, out_specs=out_specs, out_shape=out_shape,
        scratch_shapes=[pltpu.VMEM((tm, d), BF16)],
        compiler_params=_params("parallel", "parallel", "arbitrary"),
        name=name,
    )(*args)
    return res if w_extra is not None else res[0]


def _resln_kernel(*refs, has_bias):
    it = iter(refs)
    a_ref, w_ref = next(it), next(it)
    b_ref = next(it) if has_bias else None
    x_ref, gate_ref, g_ref, beta_ref, o_ref = next(it), next(it), next(it), next(it), next(it)
    y = jnp.dot(a_ref[0], w_ref[...], preferred_element_type=F32)
    if has_bias:
        y = y + b_ref[...]
    o_ref[0] = _res_ln(x_ref[0], gate_ref[0], y, g_ref[...], beta_ref[...])


def matmul_res_ln(a, w, bias, x, modl, ctx, k_gate, ln_g, ln_b, *, name):
    bsz, seq, k = a.shape
    d = x.shape[-1]
    tm = min(seq, 512)
    row = lambda b, i: (b, i, 0)
    const = lambda b, i: (0, 0)
    in_specs = [pl.BlockSpec((1, tm, k), row), pl.BlockSpec((k, d), const)]
    args = [a, w]
    if bias is not None:
        in_specs.append(pl.BlockSpec((1, d), const))
        args.append(bias)
    in_specs += [pl.BlockSpec((1, tm, d), row), _mod_spec(k_gate, ctx),
                 pl.BlockSpec((1, d), const), pl.BlockSpec((1, d), const)]
    args += [x, modl, ln_g, ln_b]
    return pl.pallas_call(
        functools.partial(_resln_kernel, has_bias=bias is not None),
        grid=(bsz, seq // tm),
        in_specs=in_specs,
        out_specs=pl.BlockSpec((1, tm, d), row),
        out_shape=jax.ShapeDtypeStruct((bsz, seq, d), F32),
        compiler_params=_params("parallel", "parallel"),
        name=name,
    )(*args)


def _fill_halo(buf_ref, cur_ref, prev_ref, next_ref, halo, tl):
    i = pl.program_id(1)
    last = pl.num_programs(1) - 1
    buf_ref[0:halo] = jnp.where(i > 0, prev_ref[0].astype(F32), 0.0)
    buf_ref[halo:halo + tl] = cur_ref[0].astype(F32)
    buf_ref[halo + tl:halo + tl + halo] = jnp.where(i < last, next_ref[0].astype(F32), 0.0)


def _halo_specs(seq, tl, halo, width, col_block):
    per = tl // halo
    last = seq // halo - 1
    def prev_map(b, i, *rest):
        return (b, jnp.maximum(i * per - 1, 0), col_block(*rest))
    def next_map(b, i, *rest):
        return (b, jnp.minimum((i + 1) * per, last), col_block(*rest))
    return pl.BlockSpec((1, halo, width), prev_map), pl.BlockSpec((1, halo, width), next_map)


def _xbc_conv_kernel(cur_ref, prev_ref, next_ref, w_ref, b_ref, o_ref, buf_ref, *, taps, halo, tl):
    _fill_halo(buf_ref, cur_ref, prev_ref, next_ref, halo, tl)
    acc = jnp.broadcast_to(b_ref[...], (tl, b_ref.shape[-1]))
    for k in range(taps):
        off = halo - taps // 2 + k
        acc = acc + w_ref[k:k + 1, :] * buf_ref[off:off + tl, :]
    o_ref[0] = _silu(acc).astype(o_ref.dtype)


def xbc_conv(proj, conv_w, conv_b):
    bsz, seq, _ = proj.shape
    taps, width = conv_w.shape
    tc = 512
    tl = min(seq, 512)
    halo = PACKED_ROWS
    first = COL_XBC // tc
    prev_spec, next_spec = _halo_specs(seq, tl, halo, tc, lambda j: first + j)
    return pl.pallas_call(
        functools.partial(_xbc_conv_kernel, taps=taps, halo=halo, tl=tl),
        grid=(bsz, seq // tl, width // tc),
        in_specs=[pl.BlockSpec((1, tl, tc), lambda b, i, j: (b, i, first + j)), prev_spec, next_spec,
                  pl.BlockSpec((taps, tc), lambda b, i, j: (0, j)),
                  pl.BlockSpec((1, tc), lambda b, i, j: (0, j))],
        out_specs=pl.BlockSpec((1, tl, tc), lambda b, i, j: (b, i, j)),
        out_shape=jax.ShapeDtypeStruct((bsz, seq, width), ACT),
        scratch_shapes=[pltpu.VMEM((tl + 2 * halo, tc), F32)],
        compiler_params=_params("parallel", "parallel", "parallel"),
        name="xbc_conv",
    )(proj, proj, proj, conv_w, conv_b)


def _conf_conv_kernel(cur_ref, prev_ref, next_ref, w_ref, b_ref, g_ref, beta_ref, o_ref, buf_ref, ph_ref,
                      *, taps, halo, tl):
    _fill_halo(buf_ref, cur_ref, prev_ref, next_ref, halo, tl)
    rows = tl + 2 * halo
    full = buf_ref[...]
    for p in range(1, SUBLANES):
        ph_ref[p - 1] = pltpu.roll(full, rows - p, axis=0)
    acc = jnp.broadcast_to(b_ref[...], (tl, b_ref.shape[-1]))
    for k in range(taps):
        off = halo - taps // 2 + k
        p = off % SUBLANES
        base = off - p
        win = buf_ref[base:base + tl, :] if p == 0 else ph_ref[p - 1, base:base + tl, :]
        acc = acc + w_ref[k:k + 1, :] * win
    mu = jnp.mean(acc, axis=-1, keepdims=True)
    ac = acc - mu
    var = jnp.mean(ac * ac, axis=-1, keepdims=True)
    u = ac * lax.rsqrt(var + EPS) * g_ref[...] + beta_ref[...]
    o_ref[0] = _silu(u).astype(o_ref.dtype)


def conf_conv(glu, dw_w, dw_b, ln_g, ln_b):
    bsz, seq, d = glu.shape
    taps = dw_w.shape[0]
    tl = min(seq, 256)
    halo = PACKED_ROWS
    prev_spec, next_spec = _halo_specs(seq, tl, halo, d, lambda: 0)
    const = lambda b, i: (0, 0)
    return pl.pallas_call(
        functools.partial(_conf_conv_kernel, taps=taps, halo=halo, tl=tl),
        grid=(bsz, seq // tl),
        in_specs=[pl.BlockSpec((1, tl, d), lambda b, i: (b, i, 0)), prev_spec, next_spec,
                  pl.BlockSpec((taps, d), const), pl.BlockSpec((1, d), const),
                  pl.BlockSpec((1, d), const), pl.BlockSpec((1, d), const)],
        out_specs=pl.BlockSpec((1, tl, d), lambda b, i: (b, i, 0)),
        out_shape=jax.ShapeDtypeStruct((bsz, seq, d), BF16),
        scratch_shapes=[pltpu.VMEM((tl + 2 * halo, d), F32),
                        pltpu.VMEM((SUBLANES - 1, tl + 2 * halo, d), F32)],
        compiler_params=_params("parallel", "parallel"),
        name="conf_conv",
    )(glu, glu, glu, dw_w, dw_b, ln_g, ln_b)


def _ssd_kernel(xs_ref, b_ref, c_ref, dt_ref, dtt_ref, dtb_ref, dtbt_ref, alog_ref, alogt_ref, dsk_ref, s0_ref,
                *rest, rev, direction, add_skip, need_out, init=True, final=True):
    y_ref, sf_ref, st_ref = rest if need_out else (None,) + rest
    T = SSD_CHUNK
    H, P, N = SSD_HEADS, SSD_HEAD_DIM, SSD_STATE
    GW = SSD_INNER // SSD_GROUPS

    if init:
        @pl.when(pl.program_id(1) == 0)
        def _():
            st_ref[...] = s0_ref[0]

    dt = _softplus(dt_ref[0][:, direction * H:(direction + 1) * H] + dtb_ref[...])
    a_col = dt * (-jnp.exp(alog_ref[...]))
    dt_row = _softplus(dtt_ref[0] + dtbt_ref[...])
    a_row = dt_row * (-jnp.exp(alogt_ref[...]))

    ti = lax.broadcasted_iota(jnp.int32, (T, T), 0)
    si = lax.broadcasted_iota(jnp.int32, (T, T), 1)
    valid = (si >= ti) if rev else (si <= ti)
    valid_f = valid.astype(F32)
    valid_t = ((ti >= si) if rev else (ti <= si)).astype(F32)
    acum = jnp.dot(valid_f, a_col, precision=HIGHEST, preferred_element_type=F32)
    acum_row = jnp.dot(a_row, valid_t, precision=HIGHEST, preferred_element_type=F32)
    atot = acum[0:1] if rev else acum[T - 1:T]

    e_p = (lax.broadcasted_iota(jnp.int32, (H, H * P), 1) // P
           == lax.broadcasted_iota(jnp.int32, (H, H * P), 0)).astype(BF16)
    expand = lambda t, passes: _split_dot(t, e_p, passes)
    eac_x = expand(jnp.exp(acum), 3) if need_out else None
    dtdec_x = expand(dt * jnp.exp(atot - acum), 2)
    eat_x = expand(jnp.broadcast_to(jnp.exp(atot), (SUBLANES, H)), 3)[0:1]

    xs = xs_ref[0].astype(F32)
    xdec = (xs * dtdec_x).astype(BF16)
    bm = b_ref[0].astype(BF16)
    cm = c_ref[0].astype(BF16)
    lane = lax.broadcasted_iota(jnp.int32, (T, 2 * P), 1)

    cbs, y_offs = [], []
    for g in range(SSD_GROUPS):
        bg = bm[:, g * N:(g + 1) * N]
        cg = cm[:, g * N:(g + 1) * N]
        s_g = st_ref[:, g * GW:(g + 1) * GW]
        if need_out:
            cbs.append(lax.dot_general(cg, bg, (((1,), (1,)), ((), ())), preferred_element_type=F32))
            y_offs.append(jnp.dot(cg, s_g.astype(BF16), preferred_element_type=F32))
        upd = lax.dot_general(bg, xdec[:, g * GW:(g + 1) * GW], (((0,), (0,)), ((), ())),
                              preferred_element_type=F32)
        st_ref[:, g * GW:(g + 1) * GW] = s_g * eat_x[:, g * GW:(g + 1) * GW] + upd
    pairs = []
    for pp in range(H // 2 if need_out else 0):
        l_pair = []
        for h in (2 * pp, 2 * pp + 1):
            diff = acum[:, h:h + 1] - acum_row[h:h + 1, :]
            decay = jnp.where(valid, jnp.exp(jnp.minimum(diff, 0.0)), 0.0)
            l_pair.append(decay * (cbs[h // (H // SSD_GROUPS)] * dt_row[h:h + 1, :]))
        lhs = jnp.concatenate(l_pair, axis=1).astype(BF16)
        xp = xs[:, 2 * pp * P:(2 * pp + 2) * P]
        rhs = jnp.concatenate([jnp.where(lane < P, xp, 0.0), jnp.where(lane >= P, xp, 0.0)],
                              axis=0).astype(BF16)
        pairs.append((lhs, rhs))
    if need_out:
        y_diag = jnp.concatenate([jnp.dot(lhs, rhs, preferred_element_type=F32) for lhs, rhs in pairs], axis=1)
        y = y_diag + jnp.concatenate(y_offs, axis=1) * eac_x
        if add_skip:
            y = y + dsk_ref[...] * xs
        y_ref[0] = y.astype(y_ref.dtype)

    if final:
        @pl.when(pl.program_id(1) == pl.num_programs(1) - 1)
        def _():
            sf_ref[0] = st_ref[...]


def _ssd_step_kernel(xs_ref, b_ref, c_ref, dt_ref, dtt_ref, *rest, rev, n_sub, need_out, **kw):
    T = SSD_CHUNK
    n_const = 6
    order = range(n_sub - 1, -1, -1) if rev else range(n_sub)
    for k, sub in enumerate(order):
        rows = lambda r: r.at[:, pl.ds(sub * T, T), :]
        tail = rest[n_const:]
        outs = ((rows(tail[0]),) + tuple(tail[1:])) if need_out else tail
        _ssd_kernel(rows(xs_ref), rows(b_ref), rows(c_ref), rows(dt_ref), dtt_ref.at[:, :, pl.ds(sub * T, T)],
                    *rest[:n_const], *outs, rev=rev, need_out=need_out, init=k == 0, final=k == n_sub - 1, **kw)


def ssd_scan(xbc, dt, dt_t, dt_bias, a_log, d_skip_x, s0, *, direction, add_skip, need_out=True):
    bsz, seq, _ = xbc.shape
    H = SSD_HEADS
    n_sub = SCAN_SUB if seq % (SCAN_SUB * SSD_CHUNK) == 0 else 1
    T = n_sub * SSD_CHUNK
    nc = seq // T
    rev = direction == 1
    chunk = (lambda c: nc - 1 - c) if rev else (lambda c: c)
    const = lambda b, c: (0, 0)
    y_spec = [pl.BlockSpec((1, T, SSD_INNER), lambda b, c: (b, chunk(c), 0))] if need_out else []
    y_shape = [jax.ShapeDtypeStruct((bsz, seq, SSD_INNER), ACT)] if need_out else []
    res = pl.pallas_call(
        functools.partial(_ssd_step_kernel, rev=rev, n_sub=n_sub, direction=direction, add_skip=add_skip,
                          need_out=need_out),
        grid=(bsz, nc),
        in_specs=[pl.BlockSpec((1, T, SSD_INNER), lambda b, c: (b, chunk(c), 0)),
                  pl.BlockSpec((1, T, SSD_BC), lambda b, c: (b, chunk(c), SSD_INNER // SSD_BC)),
                  pl.BlockSpec((1, T, SSD_BC), lambda b, c: (b, chunk(c), SSD_INNER // SSD_BC + 1)),
                  pl.BlockSpec((1, T, DT_PAD), lambda b, c: (b, chunk(c), 0)),
                  pl.BlockSpec((1, H, T), lambda b, c: (b, direction, chunk(c))),
                  pl.BlockSpec((1, H), const), pl.BlockSpec((H, 1), const),
                  pl.BlockSpec((1, H), const), pl.BlockSpec((H, 1), const),
                  pl.BlockSpec((1, SSD_INNER), const),
                  pl.BlockSpec((1, SSD_STATE, SSD_INNER), lambda b, c: (b, 0, 0))],
        out_specs=y_spec + [pl.BlockSpec((1, SSD_STATE, SSD_INNER), lambda b, c: (b, 0, 0))],
        out_shape=y_shape + [jax.ShapeDtypeStruct((bsz, SSD_STATE, SSD_INNER), F32)],
        scratch_shapes=[pltpu.VMEM((SSD_STATE, SSD_INNER), F32)],
        compiler_params=_params("parallel", "arbitrary"),
        name=f"ssd_scan_d{direction}",
    )(xbc, xbc, xbc, dt, dt_t, dt_bias[None, :], dt_bias[:, None], a_log[None, :], a_log[:, None], d_skip_x, s0)
    return (res[0], res[1]) if need_out else (None, res[0])


def _gla_kernel(q_ref, f_ref, v_ref, lbraw_ref, s0_ref, *rest, rev, first, layer, need_out, init=True, final=True):
    o_ref, sf_ref, st_ref, c_scr = rest if need_out else (None,) + rest
    T, SB = HG_CHUNK, HG_SUB
    H, DK, DV = HG_HEADS, HG_DK, HG_DV
    W = H * DK
    NT = (((1,), (1,)), ((), ()))
    TN = (((0,), (0,)), ((), ()))

    if init:
        @pl.when(pl.program_id(1) == 0)
        def _():
            st_ref[...] = s0_ref[0]

    a2 = f_ref[0].astype(F32) * LOG2E
    log_sig = jnp.minimum(a2, 0.0) - jnp.log(1.0 + jnp.exp2(-jnp.abs(a2))) * LOG2E
    log_sig_neg = log_sig - a2
    if first:
        lf, log_k = log_sig, log_sig_neg
    else:
        raw = lbraw_ref[...]
        pr = jnp.exp(raw - jnp.max(raw, axis=0, keepdims=True))
        pr = pr / jnp.sum(pr, axis=0, keepdims=True)
        lb = jnp.sum(pr[0:layer + 1], axis=0, keepdims=True) - pr[0:1]
        lf = jnp.log(lb + (1.0 - lb) * jnp.exp2(log_sig)) * LOG2E
        log_k = jnp.log(1.0 - lb) * LOG2E + log_sig_neg
    kk = jnp.exp2(log_k)
    q = _silu(q_ref[0].astype(F32))
    v_bf = v_ref[0].astype(BF16)

    ti = lax.broadcasted_iota(jnp.int32, (T, T), 0)
    si = lax.broadcasted_iota(jnp.int32, (T, T), 1)
    valid_f = ((si >= ti) if rev else (si <= ti)).astype(F32)
    acum = jnp.dot(valid_f, lf, precision=HIGHEST, preferred_element_type=F32)
    atot = acum[0:1] if rev else acum[T - 1:T]
    kd = (kk * jnp.exp2(atot - acum)).astype(BF16)
    head = lambda t, h: t[:, h * DK:(h + 1) * DK]

    def write_final_state():
        if final:
            @pl.when(pl.program_id(1) == pl.num_programs(1) - 1)
            def _():
                sf_ref[0] = st_ref[...]

    if not need_out:
        for h in range(H):
            upd_h = lax.dot_general(head(v_bf, h), head(kd, h), TN, preferred_element_type=F32)
            st_ref[h] = st_ref[h] * jnp.exp2(head(atot, h)) + upd_h
        write_final_state()
        return

    qe = (q * jnp.exp2(acum)).astype(BF16)
    c_src = acum - log_k
    for h in range(H):
        c_scr[h] = head(c_src, h)

    level_p = []
    b = T // 2
    while b >= SB:
        q_parts, k_parts = [], []
        zeros = jnp.zeros((b, W), F32)
        for c0 in range(0, T, 2 * b):
            lo, hi = slice(c0, c0 + b), slice(c0 + b, c0 + 2 * b)
            src, tgt = (hi, lo) if rev else (lo, hi)
            ref_row = c0 + b if rev else c0 + b - 1
            r = acum[ref_row:ref_row + 1]
            q_t = q[tgt] * jnp.exp2(acum[tgt] - r)
            k_s = kk[src] * jnp.exp2(r - acum[src])
            q_parts += [q_t, zeros] if rev else [zeros, q_t]
            k_parts += [zeros, k_s] if rev else [k_s, zeros]
        q_l = jnp.concatenate(q_parts, axis=0).astype(BF16)
        k_l = jnp.concatenate(k_parts, axis=0).astype(BF16)
        level_p.append([lax.dot_general(head(q_l, h), head(k_l, h), NT, preferred_element_type=F32)
                        for h in range(H)])
        b //= 2

    inter = [lax.dot_general(head(qe, h), st_ref[h].astype(BF16), NT, preferred_element_type=F32) for h in range(H)]
    upd = [lax.dot_general(head(v_bf, h), head(kd, h), TN, preferred_element_type=F32) for h in range(H)]
    for h in range(H):
        st_ref[h] = st_ref[h] * jnp.exp2(head(atot, h)) + upd[h]

    row = lax.broadcasted_iota(jnp.int32, (SB, T), 0)
    lane = lax.broadcasted_iota(jnp.int32, (SB, T), 1)
    diag_p = [[] for _ in range(H)]
    for lo in range(0, T, SB):
        a_i, q_i = acum[lo:lo + SB], q[lo:lo + SB]
        p_h = [jnp.zeros((SB, T), F32) for _ in range(H)]
        for s in range(SB):
            here = (lane == lo + s) & ((row <= s) if rev else (row >= s))
            for h in range(H):
                c_s = jnp.broadcast_to(c_scr[h, lo + s:lo + s + 1, :], (SB, DK))
                w = head(q_i, h) * jnp.exp2(head(a_i, h) - c_s)
                p_h[h] = jnp.where(here, jnp.sum(w, axis=-1, keepdims=True), p_h[h])
        for h in range(H):
            diag_p[h].append(p_h[h])

    same = []
    b = T // 4
    while b >= SB:
        shift = (2 * b).bit_length() - 1
        same.append((ti >> shift) == (si >> shift))
        b //= 2
    out = []
    for h in range(H):
        p = level_p[0][h]
        for li, m in enumerate(same):
            p = jnp.where(m, level_p[li + 1][h], p)
        p = p + jnp.concatenate(diag_p[h], axis=0)
        out.append(jnp.dot(p.astype(BF16), head(v_bf, h), preferred_element_type=F32) + inter[h])
    o_ref[0] = jnp.concatenate(out, axis=1).astype(o_ref.dtype)
    write_final_state()


def _gla_step_kernel(q_ref, f_ref, v_ref, lbraw_ref, s0_ref, *rest, rev, n_sub, need_out, **kw):
    T = HG_CHUNK
    order = range(n_sub - 1, -1, -1) if rev else range(n_sub)
    for k, sub in enumerate(order):
        rows = lambda r: r.at[:, pl.ds(sub * T, T), :]
        outs = ((rows(rest[0]),) + tuple(rest[1:])) if need_out else rest
        _gla_kernel(rows(q_ref), rows(f_ref), rows(v_ref), lbraw_ref, s0_ref, *outs, rev=rev, need_out=need_out,
                    init=k == 0, final=k == n_sub - 1, **kw)


def gla_scan(proj, lb_raw, s0, *, direction, first, layer, need_out=True):
    bsz, seq, _ = proj.shape
    H = HG_HEADS
    n_sub = SCAN_SUB if seq % (SCAN_SUB * HG_CHUNK) == 0 else 1
    T = n_sub * HG_CHUNK
    nc = seq // T
    rev = direction == 1
    chunk = (lambda c: nc - 1 - c) if rev else (lambda c: c)
    o_spec = [pl.BlockSpec((1, T, HG_INNER), lambda b, c: (b, chunk(c), 0))] if need_out else []
    o_shape = [jax.ShapeDtypeStruct((bsz, seq, HG_INNER), ACT)] if need_out else []
    res = pl.pallas_call(
        functools.partial(_gla_step_kernel, rev=rev, n_sub=n_sub, first=first, layer=layer, need_out=need_out),
        grid=(bsz, nc),
        in_specs=[pl.BlockSpec((1, T, HG_INNER), lambda b, c: (b, chunk(c), COL_Q)),
                  pl.BlockSpec((1, T, HG_INNER), lambda b, c: (b, chunk(c), COL_F + direction)),
                  pl.BlockSpec((1, T, HG_INNER), lambda b, c: (b, chunk(c), COL_V)),
                  pl.BlockSpec(lb_raw.shape, lambda b, c: (0, 0)),
                  pl.BlockSpec((1, H, HG_DV, HG_DK), lambda b, c: (b, 0, 0, 0))],
        out_specs=o_spec + [pl.BlockSpec((1, H, HG_DV, HG_DK), lambda b, c: (b, 0, 0, 0))],
        out_shape=o_shape + [jax.ShapeDtypeStruct((bsz, H, HG_DV, HG_DK), F32)],
        scratch_shapes=[pltpu.VMEM((H, HG_DV, HG_DK), F32),
                        pltpu.VMEM((H, HG_CHUNK, HG_DK), F32)],
        compiler_params=_params("parallel", "arbitrary"),
        name=f"gla_scan_d{direction}",
    )(proj, proj, proj, lb_raw, s0)
    return (res[0], res[1]) if need_out else (None, res[0])


def _mixout_kernel(yf_ref, yb_ref, z_ref, of_ref, ob_ref, g_ref, nws_ref, nwh_ref, w_ref,
                   x_ref, gate_ref, lg_ref, lb_ref, o_ref):
    def group_rms(t, groups):
        width = t.shape[-1] // groups
        parts = []
        for gi in range(groups):
            tg = t[:, gi * width:(gi + 1) * width]
            ms = jnp.mean(tg * tg, axis=-1, keepdims=True)
            parts.append(tg * lax.rsqrt(ms + EPS))
        return jnp.concatenate(parts, axis=1)

    f32 = lambda r: r[0].astype(F32)
    ys = group_rms((f32(yf_ref) + f32(yb_ref)) * _silu(f32(z_ref)), SSD_GROUPS) * nws_ref[...]
    os_ = group_rms(f32(of_ref) + f32(ob_ref), HG_HEADS) * nwh_ref[...] * _silu(f32(g_ref))
    a = jnp.concatenate([ys, os_], axis=1).astype(BF16)
    y = jnp.dot(a, w_ref[...], preferred_element_type=F32)
    o_ref[0] = _res_ln(x_ref[0], gate_ref[0], y, lg_ref[...], lb_ref[...])


def mix_out(y_f, y_b, o_f, o_b, proj, ssd_nw, hg_nw, w_out, x, modl, ctx, ln_g, ln_b):
    bsz, seq, d = x.shape
    tm = min(seq, 256)
    row = lambda b, i: (b, i, 0)
    const = lambda b, i: (0, 0)
    act = pl.BlockSpec((1, tm, d), row)
    return pl.pallas_call(
        _mixout_kernel,
        grid=(bsz, seq // tm),
        in_specs=[act, act, pl.BlockSpec((1, tm, d), lambda b, i: (b, i, COL_Z)), act, act,
                  pl.BlockSpec((1, tm, d), lambda b, i: (b, i, COL_G)),
                  pl.BlockSpec((1, d), const), pl.BlockSpec((1, d), const),
                  pl.BlockSpec(w_out.shape, const), act, _mod_spec(2, ctx),
                  pl.BlockSpec((1, d), const), pl.BlockSpec((1, d), const)],
        out_specs=act,
        out_shape=jax.ShapeDtypeStruct((bsz, seq, d), F32),
        compiler_params=_params("parallel", "parallel"),
        name="mix_out",
    )(y_f, y_b, proj, o_f, o_b, proj, ssd_nw, hg_nw, w_out, x, modl, ln_g, ln_b)


def _ffn_kernel(x_ref, xp_ref, xn_ref, sh_ref, sc_ref, gate_ref, wv0_ref, wg0_ref, wv_ref, wg_ref, cw_ref, cb_ref,
                wd_ref, lg_ref, lb_ref, o_ref, xm_ref, g_a, g_b, v_a, v_b, c0_ref, c2_ref, acc_ref,
                *, rows, cols, tl, halo, rb, n_tiles):
    i, j = pl.program_id(1), pl.program_id(2)
    last_i, last_j = pl.num_programs(1) - 1, pl.num_programs(2) - 1
    pad = SUBLANES
    ext = tl + 2 * halo
    tc = wd_ref.shape[0]

    n_parts = tl // rb

    def up_project(wv, wg, g_dst, v_dst, part=None):
        parts = range(n_parts) if part is None else (part,)
        for k in parts:
            v_lo, g_lo, g_n = k * rb, k * (ext // n_parts), ext // n_parts
            v_dst[v_lo:v_lo + rb, :] = jnp.dot(xm_ref[halo + v_lo:halo + v_lo + rb, :], wv[...],
                                               preferred_element_type=F32)
            g_dst[pad + g_lo:pad + g_lo + g_n, :] = jnp.dot(xm_ref[g_lo:g_lo + g_n, :], wg[...],
                                                             preferred_element_type=F32)
        if halo and 0 in parts:
            g_dst[pad:pad + halo, :] = jnp.where(i > 0, g_dst[pad:pad + halo, :], 0.0)
        if halo and n_parts - 1 in parts:
            top = pad + halo + tl
            g_dst[top:top + halo, :] = jnp.where(i < last_i, g_dst[top:top + halo, :], 0.0)

    @pl.when(j == 0)
    def _():
        scale, shift = 1.0 + sc_ref[0], sh_ref[0]
        xm_ref[halo:halo + tl, :] = (x_ref[0] * scale + shift).astype(BF16)
        if halo:
            xm_ref[0:halo, :] = (xp_ref[0] * scale + shift).astype(BF16)
            xm_ref[halo + tl:ext, :] = (xn_ref[0] * scale + shift).astype(BF16)
        for g_buf in (g_a, g_b):
            g_buf[0:pad, :] = jnp.zeros((pad, tc), F32)
            g_buf[pad + ext:pad + ext + pad, :] = jnp.zeros((pad, tc), F32)
        acc_ref[...] = jnp.zeros_like(acc_ref)
        up_project(wv0_ref, wg0_ref, g_a, v_a)

    def step(g_cur, v_cur, g_nxt, v_nxt, with_up=True):
        n = rb + 2 * pad
        for t0 in range(0, tl, rb):
            if with_up:
                up_project(wv_ref, wg_ref, g_nxt, v_nxt, part=t0 // rb)
            e0 = halo + t0
            mid = g_cur[e0:e0 + n, :]
            col_sums = []
            for k in range(3):
                c = cw_ref[3 + k:4 + k, :] * mid
                if rows > 1:
                    c = c + cw_ref[k:k + 1, :] * g_cur[e0 - cols:e0 - cols + n, :]
                    c = c + cw_ref[6 + k:7 + k, :] * g_cur[e0 + cols:e0 + cols + n, :]
                col_sums.append(c)
            c0_ref[...] = col_sums[0]
            c2_ref[...] = col_sums[2]
            col = (lax.broadcasted_iota(jnp.int32, (rb, 1), 0) + t0) & (cols - 1)
            u = (col_sums[1][pad:pad + rb] + cb_ref[...]
                 + jnp.where(col > 0, c0_ref[pad - 1:pad - 1 + rb, :], 0.0)
                 + jnp.where(col < cols - 1, c2_ref[pad + 1:pad + 1 + rb, :], 0.0))
            act = (u * (1.0 + lax.erf(u * (2.0 ** -0.5)))) * v_cur[t0:t0 + rb, :]
            acc_ref[t0:t0 + rb, :] += jnp.dot(act.astype(BF16), wd_ref[...], preferred_element_type=F32)

    @pl.when(jnp.logical_and(j % 2 == 0, j < last_j))
    def _():
        step(g_a, v_a, g_b, v_b)

    @pl.when(jnp.logical_and(j % 2 == 1, j < last_j))
    def _():
        step(g_b, v_b, g_a, v_a)

    @pl.when(j == last_j)
    def _():
        if (n_tiles - 1) % 2 == 0:
            step(g_a, v_a, g_b, v_b, with_up=False)
        else:
            step(g_b, v_b, g_a, v_a, with_up=False)
        o_ref[0] = _res_ln(x_ref[0], gate_ref[0], acc_ref[...], lg_ref[...], lb_ref[...])


def _ffn_block(x, modl, ctx, w_up, conv_w, conv_b, w_down, ln_g, ln_b, rows, cols):
    bsz, seq, d = x.shape
    assert seq == rows * cols and cols & (cols - 1) == 0
    tc = 256
    nj = D_FF // tc
    tl = min(seq, 1024)
    assert tl % cols == 0
    halo = LANES if rows > 1 else 0
    assert halo == 0 or (halo > cols and tl % halo == 0)
    hb = halo if halo else SUBLANES
    per = tl // hb
    last_blk = seq // hb - 1
    rb = min(tl, 256)
    ext = tl + 2 * halo
    const = lambda b, i, j: (0, 0)
    nxt = lambda j: jnp.minimum(j + 1, nj - 1)
    return pl.pallas_call(
        functools.partial(_ffn_kernel, rows=rows, cols=cols, tl=tl, halo=halo, rb=rb, n_tiles=nj),
        grid=(bsz, seq // tl, nj),
        in_specs=[pl.BlockSpec((1, tl, d), lambda b, i, j: (b, i, 0)),
                  pl.BlockSpec((1, hb, d), lambda b, i, j: (b, jnp.maximum(i * per - 1, 0), 0)),
                  pl.BlockSpec((1, hb, d), lambda b, i, j: (b, jnp.minimum((i + 1) * per, last_blk), 0)),
                  _mod_spec(3, ctx), _mod_spec(4, ctx), _mod_spec(5, ctx),
                  pl.BlockSpec((d, tc), lambda b, i, j: (0, 0)),
                  pl.BlockSpec((d, tc), lambda b, i, j: (0, nj)),
                  pl.BlockSpec((d, tc), lambda b, i, j: (0, nxt(j))),
                  pl.BlockSpec((d, tc), lambda b, i, j: (0, nj + nxt(j))),
                  pl.BlockSpec((9, tc), lambda b, i, j: (0, j)),
                  pl.BlockSpec((1, tc), lambda b, i, j: (0, j)),
                  pl.BlockSpec((tc, d), lambda b, i, j: (j, 0)),
                  pl.BlockSpec((1, d), const), pl.BlockSpec((1, d), const)],
        out_specs=pl.BlockSpec((1, tl, d), lambda b, i, j: (b, i, 0)),
        out_shape=jax.ShapeDtypeStruct((bsz, seq, d), F32),
        scratch_shapes=[pltpu.VMEM((ext, d), BF16),
                        pltpu.VMEM((ext + 2 * SUBLANES, tc), F32), pltpu.VMEM((ext + 2 * SUBLANES, tc), F32),
                        pltpu.VMEM((tl, tc), F32), pltpu.VMEM((tl, tc), F32),
                        pltpu.VMEM((rb + 2 * SUBLANES, tc), F32),
                        pltpu.VMEM((rb + 2 * SUBLANES, tc), F32),
                        pltpu.VMEM((tl, d), F32)],
        compiler_params=_params("parallel", "parallel", "arbitrary"),
        name="ffn",
    )(x, x, x, modl, modl, modl, w_up, w_up, w_up, w_up, conv_w, conv_b, w_down, ln_g, ln_b)


def _in_proj(x, modl, ctx, w_main, w_dt, conv_w, conv_b):
    proj, dt = mod_matmul(x, modl, ctx, 0, 1, w_main, n_out=IN_MAIN, w_extra=w_dt, name="in_proj")
    xbc = xbc_conv(proj, conv_w, conv_b)
    dt_t = jnp.swapaxes(dt[:, :, :2 * SSD_HEADS], 1, 2)
    return proj, xbc, dt, dt_t


def kernel(x, c, ctx, c_ctx, ada_w, ada_b, post_ln_g, post_ln_b, mix_w_in, ssd_conv_w, ssd_conv_b, ssd_dt_bias,
           ssd_a_log, ssd_d, ssd_norm_w, hg_lb_raw, hg_norm_w, mix_w_out, conf_w1, conf_b1, conf_dw_w, conf_dw_b,
           conf_ln_g, conf_ln_b, conf_w2, conf_b2, ffn_w_up, ffn_conv_w, ffn_conv_b, ffn_w_down):
    bsz, seq, d = x.shape
    rows = seq // GRID_W
    ctx_len = ctx.shape[1]
    assert d == D_MODEL and bsz <= MOD_ROWS // 2

    cond = jnp.zeros((MOD_ROWS, d), F32).at[:bsz].set(c).at[MOD_ROWS // 2].set(c_ctx)
    mod = ada_table(cond, ada_w, ada_b).reshape(DEPTH, MOD_ROWS * N_MOD, 1, d)

    xc = ctx
    for l in range(DEPTH):
        modl = mod[l]
        ctx_next = any(j % 2 == 0 for j in range(l + 1, DEPTH))
        ln_g0, ln_b0 = post_ln_g[l, 0][None], post_ln_b[l, 0][None]
        ln_g1, ln_b1 = post_ln_g[l, 1][None], post_ln_b[l, 1][None]
        if l % 2 == 0:
            e = l // 2
            w_in = mix_w_in[e]
            o_z, o_xbc, o_dt = 0, SSD_INNER, SSD_INNER + SSD_XBC
            o_q = o_dt + 2 * SSD_HEADS
            w_main = jnp.concatenate([w_in[:, o_z:o_xbc], w_in[:, o_q:], w_in[:, o_xbc:o_dt]], axis=1).astype(BF16)
            w_dt = jnp.pad(w_in[:, o_dt:o_q], ((0, 0), (0, DT_PAD - 2 * SSD_HEADS))).astype(BF16)
            conv_w, conv_b = ssd_conv_w[e], ssd_conv_b[e][None]
            d_skip_x = jnp.repeat(ssd_d[e], SSD_HEAD_DIM)[None]
            w_out = mix_w_out[e].astype(BF16)
            lat = _in_proj(x, modl, False, w_main, w_dt, conv_w, conv_b)
            cx = _in_proj(xc, modl, True, w_main, w_dt, conv_w, conv_b)
            ys, os_ = {}, {}
            for dr in range(2):
                scan_args = dict(direction=dr, add_skip=dr == 0)
                zeros_ssd = jnp.zeros((bsz, SSD_STATE, SSD_INNER), F32)
                ys["c", dr], s_ctx = ssd_scan(cx[1], cx[2], cx[3], ssd_dt_bias[e, dr], ssd_a_log[e, dr], d_skip_x,
                                              zeros_ssd, need_out=ctx_next, **scan_args)
                ys["l", dr], _ = ssd_scan(lat[1], lat[2], lat[3], ssd_dt_bias[e, dr], ssd_a_log[e, dr], d_skip_x,
                                          s_ctx, **scan_args)
                zeros_hg = jnp.zeros((bsz, HG_HEADS, HG_DV, HG_DK), F32)
                os_["c", dr], st_ctx = gla_scan(cx[0], hg_lb_raw, zeros_hg, direction=dr, first=e == 0, layer=e,
                                                need_out=ctx_next)
                os_["l", dr], _ = gla_scan(lat[0], hg_lb_raw, st_ctx, direction=dr, first=e == 0, layer=e)
            nws, nwh = ssd_norm_w[e][None], hg_norm_w[e][None]
            x1 = mix_out(ys["l", 0], ys["l", 1], os_["l", 0], os_["l", 1], lat[0], nws, nwh, w_out,
                         x, modl, False, ln_g0, ln_b0)
            if ctx_next:
                xc1 = mix_out(ys["c", 0], ys["c", 1], os_["c", 0], os_["c", 1], cx[0], nws, nwh, w_out,
                              xc, modl, True, ln_g0, ln_b0)
        else:
            o = l // 2
            w1 = conf_w1[o].astype(BF16)
            w2 = conf_w2[o].astype(BF16)
            conv_args = (conf_dw_w[o], conf_dw_b[o][None], conf_ln_g[o][None], conf_ln_b[o][None])

            def conformer(t, is_ctx):
                glu = mod_matmul(t, modl, is_ctx, 0, 1, w1, n_out=d, glu=True, bias=conf_b1[o][None], name="conf_glu")
                u = conf_conv(glu, *conv_args)
                return matmul_res_ln(u, w2, conf_b2[o][None], t, modl, is_ctx, 2, ln_g0, ln_b0, name="conf_out")

            x1 = conformer(x, False)
            if ctx_next:
                xc1 = conformer(xc, True)
        ffn_args = (ffn_w_up[l].astype(BF16), ffn_conv_w[l].reshape(9, D_FF), ffn_conv_b[l][None],
                    (0.5 * ffn_w_down[l]).astype(BF16), ln_g1, ln_b1)
        x = _ffn_block(x1, modl, False, *ffn_args, rows, GRID_W)
        if ctx_next:
            xc = _ffn_block(xc1, modl, True, *ffn_args, 1, ctx_len)
    return x
```
